```python
import jax, jax.numpy as jnp
from jax import lax
import numpy as np

D_MODEL = 2048
BATCH = 4
SEQ = 4096
DEPTH = 1

CHUNK = 64
Q_BLOCK = 128
N_MEM = 256
EPS = 1e-6
SB_WIDTH = D_MODEL // 2
SB_HEAD_DIM = 128
SB_HEADS = SB_WIDTH // SB_HEAD_DIM
MLA_WIDTH = D_MODEL - SB_WIDTH
MLA_NOPE = 128
MLA_ROPE = 64
MLA_V = 128
MLA_HEADS = MLA_WIDTH // MLA_V
Q_LORA = D_MODEL // 4
KV_LORA = D_MODEL // 8
ROPE_THETA = 10000.0
MEM_HEADS = 4
MEM_HEAD_DIM = D_MODEL // MEM_HEADS
D_FF = 4 * D_MODEL
IN_COLS = 3 * SB_WIDTH + Q_LORA + KV_LORA + MLA_ROPE
SPLITS = (SB_WIDTH, 2 * SB_WIDTH, 3 * SB_WIDTH, 3 * SB_WIDTH + Q_LORA, 3 * SB_WIDTH + Q_LORA + KV_LORA)

kernel_name = 'hybrid_stickbreaking_mla_block'


def rmsnorm(x, g):
    xf = x.astype(jnp.float32)
    y = xf * lax.rsqrt(jnp.mean(xf * xf, axis=-1, keepdims=True) + EPS)
    return (y * g.astype(jnp.float32)).astype(x.dtype)


def rope_tables(positions):
    half = MLA_ROPE // 2
    inv_freq = ROPE_THETA ** (-jnp.arange(half, dtype=jnp.float32) / half)
    ang = positions.astype(jnp.float32)[..., None] * inv_freq
    return jnp.cos(ang), jnp.sin(ang)


def apply_rope(x, cos, sin):
    half = MLA_ROPE // 2
    xf = x.astype(jnp.float32)
    x1, x2 = xf[..., :half], xf[..., half:]
    return jnp.concatenate([x1 * cos - x2 * sin, x2 * cos + x1 * sin], axis=-1).astype(x.dtype)


def stick_breaking_attention(q, k, v):
    S = q.shape[1]
    scale = q.shape[-1] ** -0.5
    outs = []
    for start in range(0, S, Q_BLOCK):
        end = start + Q_BLOCK
        z = jnp.einsum('bqhd,bkhd->bhqk', q[:, start:end], k[:, :end]).astype(jnp.float32) * scale
        t_idx = start + jnp.arange(Q_BLOCK)[:, None]
        s_idx = jnp.arange(end)[None, :]
        before = s_idx < t_idx
        log_keep = jnp.where(before, jax.nn.log_sigmoid(-z), 0.0)
        between = lax.cumsum(log_keep, axis=3, reverse=True) - log_keep
        w = jnp.where(before, jnp.exp(jax.nn.log_sigmoid(z) + between), 0.0)
        outs.append(jnp.einsum('bhqk,bkhd->bqhd', w.astype(v.dtype), v[:, :end]))
    return jnp.concatenate(outs, axis=1)


def chunk_causal_attention(q, k, v, scale):
    S = q.shape[1]
    outs = []
    for start in range(0, S, Q_BLOCK):
        end = start + Q_BLOCK
        z = jnp.einsum('bqhd,bkhd->bhqk', q[:, start:end], k[:, :end]).astype(jnp.float32) * scale
        t_chunk = (start + jnp.arange(Q_BLOCK))[:, None] // CHUNK
        s_chunk = jnp.arange(end)[None, :] // CHUNK
        z = jnp.where(s_chunk <= t_chunk, z, -1e30)
        p = jax.nn.softmax(z, axis=-1)
        outs.append(jnp.einsum('bhqk,bkhd->bqhd', p.astype(v.dtype), v[:, :end]))
    return jnp.concatenate(outs, axis=1)


def setup_inputs(seed: int = 0) -> dict:
    key = jax.random.key(seed)
    ks = jax.random.split(key, 24)

    def w(k, shape, fan_in):
        return jax.random.normal(k, shape, jnp.float32) * (fan_in ** -0.5)

    def gain(k, shape):
        return 1.0 + 0.01 * jax.random.normal(k, shape, jnp.float32)

    L = DEPTH
    offset = jax.random.randint(ks[2], (BATCH, 1), 0, 100000, dtype=jnp.int32)
    positions = offset + jnp.arange(SEQ, dtype=jnp.int32)[None, :]
    return {
        'x': jax.random.normal(ks[0], (BATCH, SEQ, D_MODEL), jnp.float32),
        'mem': jax.random.normal(ks[1], (BATCH, N_MEM, D_MODEL), jnp.float32),
        'positions': positions,
        'norm_mix_g': gain(ks[3], (L, D_MODEL)),
        'w_in': w(ks[4], (L, D_MODEL, IN_COLS), D_MODEL),
        'q_norm_g': gain(ks[5], (L, Q_LORA)),
        'w_uq': w(ks[6], (L, Q_LORA, MLA_HEADS * (MLA_NOPE + MLA_ROPE)), Q_LORA),
        'kv_norm_g': gain(ks[7], (L, KV_LORA)),
        'w_ukv': w(ks[8], (L, KV_LORA, MLA_HEADS * (MLA_NOPE + MLA_V)), KV_LORA),
        'gn_sb_g': gain(ks[9], (L, SB_WIDTH)),
        'gn_mla_g': gain(ks[10], (L, MLA_WIDTH)),
        'w_out': w(ks[11], (L, D_MODEL, D_MODEL), D_MODEL),
        'norm_mem_g': gain(ks[12], (L, D_MODEL)),
        'mem_kv_norm_g': gain(ks[13], (L, D_MODEL)),
        'w_mq': w(ks[14], (L, D_MODEL, D_MODEL), D_MODEL),
        'w_mkv': w(ks[15], (L, D_MODEL, 2 * D_MODEL), D_MODEL),
        'w_mo': w(ks[16], (L, D_MODEL, D_MODEL), D_MODEL),
        'norm_ffn_g': gain(ks[17], (L, D_MODEL)),
        'w_ff1': w(ks[18], (L, D_MODEL, D_FF), D_MODEL),
        'w_ff2': w(ks[19], (L, D_FF, D_MODEL), D_FF),
        'final_norm_g': gain(ks[20], (D_MODEL,)),
    }


def reference(x, mem, positions, norm_mix_g, w_in, q_norm_g, w_uq, kv_norm_g, w_ukv, gn_sb_g, gn_mla_g, w_out, norm_mem_g, mem_kv_norm_g, w_mq, w_mkv, w_mo, norm_ffn_g, w_ff1, w_ff2, final_norm_g):
    B, S, D = x.shape
    M = mem.shape[1]
    cos, sin = rope_tables(positions)
    for l in range(DEPTH):
        h = rmsnorm(x, norm_mix_g[l])
        proj = h @ w_in[l]
        q_sb, k_sb, v_sb, c_q, c_kv, k_pe = jnp.split(proj, SPLITS, axis=-1)

        o_sb = stick_breaking_attention(
            q_sb.reshape(B, S, SB_HEADS, SB_HEAD_DIM),
            k_sb.reshape(B, S, SB_HEADS, SB_HEAD_DIM),
            v_sb.reshape(B, S, SB_HEADS, SB_HEAD_DIM)).reshape(B, S, SB_WIDTH)

        q = (rmsnorm(c_q, q_norm_g[l]) @ w_uq[l]).reshape(B, S, MLA_HEADS, MLA_NOPE + MLA_ROPE)
        q_nope, q_pe = q[..., :MLA_NOPE], q[..., MLA_NOPE:]
        q_pe = apply_rope(q_pe, cos[:, :, None, :], sin[:, :, None, :])
        k_pe = apply_rope(k_pe, cos, sin)
        kv = (rmsnorm(c_kv, kv_norm_g[l]) @ w_ukv[l]).reshape(B, S, MLA_HEADS, MLA_NOPE + MLA_V)
        k_nope, v_mla = kv[..., :MLA_NOPE], kv[..., MLA_NOPE:]
        q_mla = jnp.concatenate([q_nope, q_pe], axis=-1)
        k_mla = jnp.concatenate([k_nope, jnp.broadcast_to(k_pe[:, :, None, :], (B, S, MLA_HEADS, MLA_ROPE))], axis=-1)
        o_mla = chunk_causal_attention(q_mla, k_mla, v_mla, (MLA_NOPE + MLA_ROPE) ** -0.5).reshape(B, S, MLA_WIDTH)

        o = jnp.concatenate([rmsnorm(o_sb, gn_sb_g[l]), rmsnorm(o_mla, gn_mla_g[l])], axis=-1)
        x = x + o @ w_out[l]

        h = rmsnorm(x, norm_mem_g[l])
        m = rmsnorm(mem, mem_kv_norm_g[l])
        qm = (h @ w_mq[l]).reshape(B, S, MEM_HEADS, MEM_HEAD_DIM)
        km, vm = jnp.split(m @ w_mkv[l], 2, axis=-1)
        km = km.reshape(B, M, MEM_HEADS, MEM_HEAD_DIM)
        vm = vm.reshape(B, M, MEM_HEADS, MEM_HEAD_DIM)
        zm = jnp.einsum('bqhd,bmhd->bhqm', qm, km).astype(jnp.float32) * (MEM_HEAD_DIM ** -0.5)
        pm = jax.nn.softmax(zm, axis=-1)
        om = jnp.einsum('bhqm,bmhd->bqhd', pm.astype(vm.dtype), vm).reshape(B, S, D)
        x = x + om @ w_mo[l]

        h = rmsnorm(x, norm_ffn_g[l])
        x = x + jnp.square(jax.nn.relu(h @ w_ff1[l])) @ w_ff2[l]
    return rmsnorm(x, final_norm_g)
```

```python
import functools

import jax
import jax.numpy as jnp
from jax import lax
from jax.experimental import pallas as pl
from jax.experimental.pallas import tpu as pltpu

F32 = jnp.float32
BF16 = jnp.bfloat16

EPS = 1e-6
CHUNK = 64
SB_HEAD_DIM = 128
MLA_NOPE = 128
MLA_ROPE = 64
MLA_V = 128
MLA_QK_PAD = 256
ROPE_THETA = 10000.0
MEM_HEADS = 4
LANES = 128
VMEM_LIMIT = 56 * 1024 * 1024


def _rms_scale(xf, width):
    return lax.rsqrt(jnp.sum(xf * xf, axis=-1, keepdims=True) * (1.0 / width) + EPS)


def _dot(a, b):
    return jnp.dot(a, b, preferred_element_type=F32)


def _dot_nt(a, b):
    return lax.dot_general(a, b, (((1,), (1,)), ((), ())), preferred_element_type=F32)


def _norm_matmul_kernel(*refs, n_parts, widths, normed, has_scale, has_res):
    pos = 0
    x_refs = refs[pos:pos + n_parts]; pos += n_parts
    g_refs = ()
    if normed:
        g_refs = refs[pos:pos + n_parts]; pos += n_parts
    w_ref = refs[pos]; pos += 1
    scale_ref = None
    if has_scale:
        scale_ref = refs[pos]; pos += 1
    res_ref = None
    if has_res:
        res_ref = refs[pos]; pos += 1
    o_ref = refs[pos]; pos += 1
    h_ref = refs[pos]

    @pl.when(pl.program_id(1) == 0)
    def _():
        off = 0
        for p in range(n_parts):
            width = widths[p]
            xf = x_refs[p][...].astype(F32)
            if normed:
                xf = xf * _rms_scale(xf, width) * g_refs[p][...]
            h_ref[:, off:off + width] = xf.astype(BF16)
            off += width

    acc = _dot(h_ref[...], w_ref[...])
    if has_scale:
        acc = acc * scale_ref[...]
    if has_res:
        acc = acc + res_ref[...]
    o_ref[...] = acc.astype(o_ref.dtype)


def _norm_matmul(parts, gains, w, *, col_scale=None, residual=None, out_dtype=BF16, tm=512, tn=512):
    T = parts[0].shape[0]
    widths = tuple(p.shape[1] for p in parts)
    K = sum(widths)
    N = w.shape[1]
    tm = min(tm, T)
    tn = min(tn, N)
    assert T % tm == 0 and N % tn == 0 and w.shape[0] == K
    normed = gains is not None
    in_specs = [pl.BlockSpec((tm, k), lambda i, j: (i, 0)) for k in widths]
    args = list(parts)
    if normed:
        in_specs += [pl.BlockSpec((1, k), lambda i, j: (0, 0)) for k in widths]
        args += [g.reshape(1, -1).astype(F32) for g in gains]
    in_specs.append(pl.BlockSpec((K, tn), lambda i, j: (0, j)))
    args.append(w)
    if col_scale is not None:
        in_specs.append(pl.BlockSpec((1, tn), lambda i, j: (0, j)))
        args.append(col_scale.reshape(1, N).astype(F32))
    if residual is not None:
        in_specs.append(pl.BlockSpec((tm, tn), lambda i, j: (i, j)))
        args.append(residual)
    kern = functools.partial(
        _norm_matmul_kernel, n_parts=len(parts), widths=widths, normed=normed,
        has_scale=col_scale is not None, has_res=residual is not None)
    return pl.pallas_call(
        kern,
        grid=(T // tm, N // tn),
        in_specs=in_specs,
        out_specs=pl.BlockSpec((tm, tn), lambda i, j: (i, j)),
        out_shape=jax.ShapeDtypeStruct((T, N), out_dtype),
        scratch_shapes=[pltpu.VMEM((tm, K), BF16)],
        compiler_params=pltpu.CompilerParams(
            dimension_semantics=("parallel", "arbitrary"), vmem_limit_bytes=VMEM_LIMIT),
    )(*args)


def _rope(xp, cos_t, sin_lo, sin_hi):
    half = MLA_ROPE // 2
    return xp * cos_t + pltpu.roll(xp, LANES - half, 1) * sin_lo + pltpu.roll(xp, half, 1) * sin_hi


def _mla_proj_kernel(x_ref, g_ref, wlat_ref, qg_ref, wuq_ref, kvg_ref, wuk_ref, wuv_ref,
                     pos_ref, freq_ref, q_ref, k_ref, v_ref, *, d_model, q_lora, kv_lora, heads, q_scale):
    xf = x_ref[...]
    h = (xf * _rms_scale(xf, d_model) * g_ref[...]).astype(BF16)
    lat = _dot(h, wlat_ref[...])
    cq = lat[:, :q_lora]
    ckv = lat[:, q_lora:q_lora + kv_lora]
    kpe = lat[:, q_lora + kv_lora:]

    half = MLA_ROPE // 2
    ang = pos_ref[...].astype(F32) * freq_ref[...]
    cos_t = jnp.cos(ang)
    sin_t = jnp.sin(ang)
    lane = lax.broadcasted_iota(jnp.int32, ang.shape, 1)
    sin_lo = jnp.where(lane < half, -sin_t, 0.0)
    sin_hi = jnp.where((lane >= half) & (lane < 2 * half), sin_t, 0.0)

    cqn = (cq * _rms_scale(cq, q_lora) * qg_ref[...]).astype(BF16)
    q = _dot(cqn, wuq_ref[...]) * q_scale
    ckvn = (ckv * _rms_scale(ckv, kv_lora) * kvg_ref[...]).astype(BF16)
    kn = _dot(ckvn, wuk_ref[...])
    v_ref[...] = _dot(ckvn, wuv_ref[...]).astype(v_ref.dtype)
    kpe_r = _rope(kpe, cos_t, sin_lo, sin_hi).astype(k_ref.dtype)
    for hd in range(heads):
        base = hd * MLA_QK_PAD
        q_ref[:, base:base + MLA_NOPE] = q[:, base:base + MLA_NOPE].astype(q_ref.dtype)
        q_ref[:, base + MLA_NOPE:base + MLA_QK_PAD] = _rope(
            q[:, base + MLA_NOPE:base + MLA_QK_PAD], cos_t, sin_lo, sin_hi).astype(q_ref.dtype)
        k_ref[:, base:base + MLA_NOPE] = kn[:, hd * MLA_NOPE:(hd + 1) * MLA_NOPE].astype(k_ref.dtype)
        k_ref[:, base + MLA_NOPE:base + MLA_QK_PAD] = kpe_r


def _mla_proj(x2d, g, wlat, qg, wuq, kvg, wuk, wuv, pos2d, freq, *, heads, tm=512):
    T, D = x2d.shape
    q_lora = qg.shape[0]
    kv_lora = kvg.shape[0]
    nlat = wlat.shape[1]
    full = lambda shape: pl.BlockSpec(shape, lambda i: (0, 0))
    kern = functools.partial(
        _mla_proj_kernel, d_model=D, q_lora=q_lora, kv_lora=kv_lora, heads=heads,
        q_scale=float((MLA_NOPE + MLA_ROPE) ** -0.5))
    return pl.pallas_call(
        kern,
        grid=(T // tm,),
        in_specs=[
            pl.BlockSpec((tm, D), lambda i: (i, 0)),
            full((1, D)),
            full((D, nlat)),
            full((1, q_lora)),
            full((q_lora, heads * MLA_QK_PAD)),
            full((1, kv_lora)),
            full((kv_lora, heads * MLA_NOPE)),
            full((kv_lora, heads * MLA_V)),
            pl.BlockSpec((tm, 1), lambda i: (i, 0)),
            full((1, LANES)),
        ],
        out_specs=[
            pl.BlockSpec((tm, heads * MLA_QK_PAD), lambda i: (i, 0)),
            pl.BlockSpec((tm, heads * MLA_QK_PAD), lambda i: (i, 0)),
            pl.BlockSpec((tm, heads * MLA_V), lambda i: (i, 0)),
        ],
        out_shape=[
            jax.ShapeDtypeStruct((T, heads * MLA_QK_PAD), BF16),
            jax.ShapeDtypeStruct((T, heads * MLA_QK_PAD), BF16),
            jax.ShapeDtypeStruct((T, heads * MLA_V), BF16),
        ],
        compiler_params=pltpu.CompilerParams(
            dimension_semantics=("parallel",), vmem_limit_bytes=VMEM_LIMIT),
    )(x2d, g.reshape(1, D), wlat, qg.reshape(1, -1), wuq, kvg.reshape(1, -1), wuk, wuv, pos2d, freq)


def _sb_attn_kernel(q_ref, k_ref, v_ref, o_ref, *, blk):
    qi = pl.program_id(2)
    q = q_ref[0]
    row = lax.broadcasted_iota(jnp.int32, (blk, blk), 0)
    col = lax.broadcasted_iota(jnp.int32, (blk, blk), 1)
    below = row > col
    suffix = below.astype(BF16)

    def step(kb, acc, run, masked):
        start = pl.multiple_of(kb * blk, blk)
        k = k_ref[0, pl.ds(start, blk), :]
        v = v_ref[0, pl.ds(start, blk), :]
        z = _dot_nt(q, k)
        sp = jnp.maximum(z, 0.0) + jnp.log(1.0 + jnp.exp(-jnp.abs(z)))
        if masked:
            sp = jnp.where(below, sp, 0.0)
        hi = sp.astype(BF16)
        lo = (sp - hi.astype(F32)).astype(BF16)
        cs = _dot(hi, suffix) + _dot(lo, suffix)
        w = jnp.exp(z - sp - cs - run)
        if masked:
            w = jnp.where(below, w, 0.0)
        acc = acc + _dot(w.astype(BF16), v)
        run = run + jnp.sum(sp, axis=-1, keepdims=True)
        return acc, run

    acc = jnp.zeros((blk, SB_HEAD_DIM), F32)
    run = jnp.zeros((blk, 1), F32)
    acc, run = step(qi, acc, run, True)

    def body(it, carry):
        return step(qi - 1 - it, carry[0], carry[1], False)

    acc, run = lax.fori_loop(0, qi, body, (acc, run))
    o_ref[0] = acc.astype(o_ref.dtype)


def _sb_attn(qkv, *, heads, blk=256):
    B, S, _ = qkv.shape
    return pl.pallas_call(
        functools.partial(_sb_attn_kernel, blk=blk),
        grid=(B, heads, S // blk),
        in_specs=[
            pl.BlockSpec((1, blk, SB_HEAD_DIM), lambda b, h, i: (b, i, h)),
            pl.BlockSpec((1, S, SB_HEAD_DIM), lambda b, h, i: (b, 0, heads + h)),
            pl.BlockSpec((1, S, SB_HEAD_DIM), lambda b, h, i: (b, 0, 2 * heads + h)),
        ],
        out_specs=pl.BlockSpec((1, blk, SB_HEAD_DIM), lambda b, h, i: (b, i, h)),
        out_shape=jax.ShapeDtypeStruct((B, S, heads * SB_HEAD_DIM), BF16),
        compiler_params=pltpu.CompilerParams(
            dimension_semantics=("parallel", "parallel", "arbitrary"), vmem_limit_bytes=VMEM_LIMIT),
    )(qkv, qkv, qkv)


def _mla_attn_kernel(q_ref, k_ref, v_ref, o_ref, *, blk):
    qi = pl.program_id(2)
    q = q_ref[0]
    row = lax.broadcasted_iota(jnp.int32, (blk, blk), 0)
    col = lax.broadcasted_iota(jnp.int32, (blk, blk), 1)
    visible = (col // CHUNK) <= (row // CHUNK)

    def step(kb, m, l, acc, masked):
        start = pl.multiple_of(kb * blk, blk)
        k = k_ref[0, pl.ds(start, blk), :]
        v = v_ref[0, pl.ds(start, blk), :]
        z = _dot_nt(q, k)
        if masked:
            z = jnp.where(visible, z, -1e30)
        m_new = jnp.maximum(m, jnp.max(z, axis=-1, keepdims=True))
        p = jnp.exp(z - m_new)
        alpha = jnp.exp(m - m_new)
        l = alpha * l + jnp.sum(p, axis=-1, keepdims=True)
        acc = alpha * acc + _dot(p.astype(BF16), v)
        return m_new, l, acc

    m = jnp.full((blk, 1), -1e30, F32)
    l = jnp.zeros((blk, 1), F32)
    acc = jnp.zeros((blk, MLA_V), F32)
    m, l, acc = step(qi, m, l, acc, True)

    def body(it, carry):
        return step(qi - 1 - it, carry[0], carry[1], carry[2], False)

    m, l, acc = lax.fori_loop(0, qi, body, (m, l, acc))
    o_ref[0] = (acc / l).astype(o_ref.dtype)


def _mla_attn(q, k, v, *, heads, blk=256):
    B, S, _ = q.shape
    return pl.pallas_call(
        functools.partial(_mla_attn_kernel, blk=blk),
        grid=(B, heads, S // blk),
        in_specs=[
            pl.BlockSpec((1, blk, MLA_QK_PAD), lambda b, h, i: (b, i, h)),
            pl.BlockSpec((1, S, MLA_QK_PAD), lambda b, h, i: (b, 0, h)),
            pl.BlockSpec((1, S, MLA_V), lambda b, h, i: (b, 0, h)),
        ],
        out_specs=pl.BlockSpec((1, blk, MLA_V), lambda b, h, i: (b, i, h)),
        out_shape=jax.ShapeDtypeStruct((B, S, heads * MLA_V), BF16),
        compiler_params=pltpu.CompilerParams(
            dimension_semantics=("parallel", "parallel", "arbitrary"), vmem_limit_bytes=VMEM_LIMIT),
    )(q, k, v)


def _mem_attn_kernel(q_ref, kv_ref, o_ref, *, heads, head_dim, d_model):
    for hd in range(heads):
        q = q_ref[0, :, hd * head_dim:(hd + 1) * head_dim]
        k = kv_ref[0, :, hd * head_dim:(hd + 1) * head_dim]
        v = kv_ref[0, :, d_model + hd * head_dim:d_model + (hd + 1) * head_dim]
        z = _dot_nt(q, k)
        p = jnp.exp(z - jnp.max(z, axis=-1, keepdims=True))
        l = jnp.sum(p, axis=-1, keepdims=True)
        o = _dot(p.astype(BF16), v) / l
        o_ref[0, :, hd * head_dim:(hd + 1) * head_dim] = o.astype(o_ref.dtype)


def _mem_attn(qm, kvm, *, heads, tq=512):
    B, S, D = qm.shape
    M = kvm.shape[1]
    return pl.pallas_call(
        functools.partial(_mem_attn_kernel, heads=heads, head_dim=D // heads, d_model=D),
        grid=(B, S // tq),
        in_specs=[
            pl.BlockSpec((1, tq, D), lambda b, i: (b, i, 0)),
            pl.BlockSpec((1, M, 2 * D), lambda b, i: (b, 0, 0)),
        ],
        out_specs=pl.BlockSpec((1, tq, D), lambda b, i: (b, i, 0)),
        out_shape=jax.ShapeDtypeStruct((B, S, D), BF16),
        compiler_params=pltpu.CompilerParams(
            dimension_semantics=("parallel", "parallel"), vmem_limit_bytes=VMEM_LIMIT),
    )(qm, kvm)


def _ffn_kernel(x_ref, g_ref, w1_ref, w2_ref, gf_ref, o_ref, h_ref, acc_ref, *, d_model):
    f = pl.program_id(1)

    @pl.when(f == 0)
    def _():
        xf = x_ref[...]
        h_ref[...] = (xf * _rms_scale(xf, d_model) * g_ref[...]).astype(BF16)
        acc_ref[...] = jnp.zeros_like(acc_ref)

    a = jnp.maximum(_dot(h_ref[...], w1_ref[...]), 0.0)
    acc_ref[...] += _dot((a * a).astype(BF16), w2_ref[...])

    @pl.when(f == pl.num_programs(1) - 1)
    def _():
        y = x_ref[...] + acc_ref[...]
        o_ref[...] = y * _rms_scale(y, d_model) * gf_ref[...]


def _ffn(x2d, g, w1, w2, gf, *, tm=512, tf=512):
    T, D = x2d.shape
    F = w1.shape[1]
    return pl.pallas_call(
        functools.partial(_ffn_kernel, d_model=D),
        grid=(T // tm, F // tf),
        in_specs=[
            pl.BlockSpec((tm, D), lambda i, f: (i, 0)),
            pl.BlockSpec((1, D), lambda i, f: (0, 0)),
            pl.BlockSpec((D, tf), lambda i, f: (0, f)),
            pl.BlockSpec((tf, D), lambda i, f: (f, 0)),
            pl.BlockSpec((1, D), lambda i, f: (0, 0)),
        ],
        out_specs=pl.BlockSpec((tm, D), lambda i, f: (i, 0)),
        out_shape=jax.ShapeDtypeStruct((T, D), F32),
        scratch_shapes=[pltpu.VMEM((tm, D), BF16), pltpu.VMEM((tm, D), F32)],
        compiler_params=pltpu.CompilerParams(
            dimension_semantics=("parallel", "arbitrary"), vmem_limit_bytes=VMEM_LIMIT),
    )(x2d, g.reshape(1, D), w1, w2, gf.reshape(1, D))


def _pad_cols(w, n):
    return jnp.pad(w, ((0, 0), (0, n - w.shape[1])))


def kernel(x, mem, positions, norm_mix_g, w_in, q_norm_g, w_uq, kv_norm_g, w_ukv, gn_sb_g, gn_mla_g, w_out, norm_mem_g, mem_kv_norm_g, w_mq, w_mkv, w_mo, norm_ffn_g, w_ff1, w_ff2, final_norm_g):
    B, S, D = x.shape
    M = mem.shape[1]
    T = B * S
    depth = w_in.shape[0]
    sb_width = gn_sb_g.shape[1]
    sb_heads = sb_width // SB_HEAD_DIM
    mla_heads = gn_mla_g.shape[1] // MLA_V
    q_lora = q_norm_g.shape[1]
    kv_lora = kv_norm_g.shape[1]
    mem_head_dim = D // MEM_HEADS

    half = MLA_ROPE // 2
    inv_freq = ROPE_THETA ** (-jnp.arange(half, dtype=F32) / half)
    freq = jnp.concatenate([inv_freq, inv_freq, jnp.zeros((LANES - 2 * half,), F32)]).reshape(1, LANES)
    pos2d = positions.reshape(T, 1)

    xc = x.reshape(T, D)
    for l in range(depth):
        w_sb = w_in[l][:, :3 * sb_width].astype(BF16)
        w_lat = _pad_cols(w_in[l][:, 3 * sb_width:], q_lora + kv_lora + LANES).astype(BF16)
        wq = w_uq[l].reshape(q_lora, mla_heads, MLA_NOPE + MLA_ROPE)
        wq = jnp.pad(wq, ((0, 0), (0, 0), (0, MLA_QK_PAD - MLA_NOPE - MLA_ROPE)))
        wq = wq.reshape(q_lora, mla_heads * MLA_QK_PAD).astype(BF16)
        wkv = w_ukv[l].reshape(kv_lora, mla_heads, MLA_NOPE + MLA_V)
        wuk = wkv[:, :, :MLA_NOPE].reshape(kv_lora, mla_heads * MLA_NOPE).astype(BF16)
        wuv = wkv[:, :, MLA_NOPE:].reshape(kv_lora, mla_heads * MLA_V).astype(BF16)
        sb_scale = jnp.concatenate([
            jnp.full((sb_width,), SB_HEAD_DIM ** -0.5, F32), jnp.ones((2 * sb_width,), F32)])
        mq_scale = jnp.full((D,), mem_head_dim ** -0.5, F32)

        qkv = _norm_matmul([xc], [norm_mix_g[l]], w_sb, col_scale=sb_scale, tn=1024)
        q_mla, k_mla, v_mla = _mla_proj(
            xc, norm_mix_g[l], w_lat, q_norm_g[l], wq, kv_norm_g[l], wuk, wuv, pos2d, freq,
            heads=mla_heads)
        o_sb = _sb_attn(qkv.reshape(B, S, 3 * sb_width), heads=sb_heads)
        o_mla = _mla_attn(q_mla.reshape(B, S, -1), k_mla.reshape(B, S, -1), v_mla.reshape(B, S, -1),
                          heads=mla_heads)
        xc = _norm_matmul(
            [o_sb.reshape(T, -1), o_mla.reshape(T, -1)], [gn_sb_g[l], gn_mla_g[l]],
            w_out[l].astype(BF16), residual=xc, out_dtype=F32, tn=1024)

        qm = _norm_matmul([xc], [norm_mem_g[l]], w_mq[l].astype(BF16), col_scale=mq_scale, tn=1024)
        kvm = _norm_matmul([mem.reshape(B * M, D)], [mem_kv_norm_g[l]], w_mkv[l].astype(BF16), tn=1024)
        om = _mem_attn(qm.reshape(B, S, D), kvm.reshape(B, M, 2 * D), heads=MEM_HEADS)
        xc = _norm_matmul([om.reshape(T, D)], None, w_mo[l].astype(BF16), residual=xc,
                          out_dtype=F32, tn=1024)

        assert depth == 1
        xc = _ffn(xc, norm_ffn_g[l], w_ff1[l].astype(BF16), w_ff2[l].astype(BF16), final_norm_g)
    return xc.reshape(B, S, D)
```

```python
import functools

import jax
import jax.numpy as jnp
from jax import lax
from jax.experimental import pallas as pl
from jax.experimental.pallas import tpu as pltpu

F32 = jnp.float32
BF16 = jnp.bfloat16

EPS = 1e-6
CHUNK = 64
SB_HEAD_DIM = 128
MLA_NOPE = 128
MLA_ROPE = 64
MLA_V = 128
MLA_QK_PAD = 256
ROPE_THETA = 10000.0
LOG2_E = 1.4426950408889634
MEM_HEADS = 4
LANES = 128
VMEM_LIMIT = 56 * 1024 * 1024


def _rms_scale(xf, width):
    return lax.rsqrt(jnp.sum(xf * xf, axis=-1, keepdims=True) * (1.0 / width) + EPS)


def _dot(a, b):
    return jnp.dot(a, b, preferred_element_type=F32)


def _dot_nt(a, b):
    return lax.dot_general(a, b, (((1,), (1,)), ((), ())), preferred_element_type=F32)


def _norm_matmul_kernel(*refs, n_parts, widths, normed, has_scale, has_res, tn):
    pos = 0
    x_refs = refs[pos:pos + n_parts]; pos += n_parts
    g_refs = ()
    if normed:
        g_refs = refs[pos:pos + n_parts]; pos += n_parts
    w_ref = refs[pos]; pos += 1
    scale_ref = None
    if has_scale:
        scale_ref = refs[pos]; pos += 1
    res_ref = None
    if has_res:
        res_ref = refs[pos]; pos += 1
    o_ref = refs[pos]

    hs = []
    for p in range(n_parts):
        xf = x_refs[p][...].astype(F32)
        if normed:
            xf = xf * _rms_scale(xf, widths[p]) * g_refs[p][...]
        hs.append(xf.astype(BF16))
    h = hs[0] if n_parts == 1 else jnp.concatenate(hs, axis=1)
    for c in range(0, o_ref.shape[1], tn):
        acc = _dot(h, w_ref[:, c:c + tn])
        if has_scale:
            acc = acc * scale_ref[:, c:c + tn]
        if has_res:
            acc = acc + res_ref[:, c:c + tn]
        o_ref[:, c:c + tn] = acc.astype(o_ref.dtype)


def _norm_matmul(parts, gains, w, *, name, col_scale=None, residual=None, out_dtype=BF16, tm=512, tn=512):
    T = parts[0].shape[0]
    widths = tuple(p.shape[1] for p in parts)
    K = sum(widths)
    N = w.shape[1]
    tm = min(tm, T)
    tn = min(tn, N)
    assert T % tm == 0 and N % tn == 0 and w.shape[0] == K
    normed = gains is not None
    in_specs = [pl.BlockSpec((tm, k), lambda i: (i, 0)) for k in widths]
    args = list(parts)
    if normed:
        in_specs += [pl.BlockSpec((1, k), lambda i: (0, 0)) for k in widths]
        args += [g.reshape(1, -1).astype(F32) for g in gains]
    in_specs.append(pl.BlockSpec((K, N), lambda i: (0, 0), pipeline_mode=pl.Buffered(1)))
    args.append(w)
    if col_scale is not None:
        in_specs.append(pl.BlockSpec((1, N), lambda i: (0, 0)))
        args.append(col_scale.reshape(1, N).astype(F32))
    if residual is not None:
        in_specs.append(pl.BlockSpec((tm, N), lambda i: (i, 0)))
        args.append(residual)
    kern = functools.partial(
        _norm_matmul_kernel, n_parts=len(parts), widths=widths, normed=normed,
        has_scale=col_scale is not None, has_res=residual is not None, tn=tn)
    return pl.pallas_call(
        kern,
        grid=(T // tm,),
        in_specs=in_specs,
        out_specs=pl.BlockSpec((tm, N), lambda i: (i, 0)),
        out_shape=jax.ShapeDtypeStruct((T, N), out_dtype),
        compiler_params=pltpu.CompilerParams(
            dimension_semantics=("parallel",), vmem_limit_bytes=VMEM_LIMIT),
        name=name,
    )(*args)


def _rope(xp, cos_t, sin_lo, sin_hi):
    half = MLA_ROPE // 2
    return xp * cos_t + pltpu.roll(xp, LANES - half, 1) * sin_lo + pltpu.roll(xp, half, 1) * sin_hi


def _mla_proj_kernel(x_ref, g_ref, wlat_ref, qg_ref, wuq_ref, kvg_ref, wuk_ref, wuv_ref,
                     pos_ref, freq_ref, q_ref, k_ref, v_ref, *, d_model, q_lora, kv_lora, heads, q_scale):
    xf = x_ref[...]
    h = (xf * _rms_scale(xf, d_model) * g_ref[...]).astype(BF16)
    lat = _dot(h, wlat_ref[...])
    cq = lat[:, :q_lora]
    ckv = lat[:, q_lora:q_lora + kv_lora]
    kpe = lat[:, q_lora + kv_lora:]

    half = MLA_ROPE // 2
    ang = pos_ref[...].astype(F32) * freq_ref[...]
    cos_t = jnp.cos(ang)
    sin_t = jnp.sin(ang)
    lane = lax.broadcasted_iota(jnp.int32, ang.shape, 1)
    sin_lo = jnp.where(lane < half, -sin_t, 0.0)
    sin_hi = jnp.where((lane >= half) & (lane < 2 * half), sin_t, 0.0)

    cqn = (cq * _rms_scale(cq, q_lora) * qg_ref[...]).astype(BF16)
    q = _dot(cqn, wuq_ref[...]) * q_scale
    ckvn = (ckv * _rms_scale(ckv, kv_lora) * kvg_ref[...]).astype(BF16)
    kn = _dot(ckvn, wuk_ref[...])
    v_ref[...] = _dot(ckvn, wuv_ref[...]).astype(v_ref.dtype)
    kpe_r = _rope(kpe, cos_t, sin_lo, sin_hi).astype(k_ref.dtype)
    for hd in range(heads):
        base = hd * MLA_QK_PAD
        q_ref[:, base:base + MLA_NOPE] = q[:, base:base + MLA_NOPE].astype(q_ref.dtype)
        q_ref[:, base + MLA_NOPE:base + MLA_QK_PAD] = _rope(
            q[:, base + MLA_NOPE:base + MLA_QK_PAD], cos_t, sin_lo, sin_hi).astype(q_ref.dtype)
        k_ref[:, base:base + MLA_NOPE] = kn[:, hd * MLA_NOPE:(hd + 1) * MLA_NOPE].astype(k_ref.dtype)
        k_ref[:, base + MLA_NOPE:base + MLA_QK_PAD] = kpe_r


def _mla_proj(x2d, g, wlat, qg, wuq, kvg, wuk, wuv, pos2d, freq, *, heads, tm=512):
    T, D = x2d.shape
    q_lora = qg.shape[0]
    kv_lora = kvg.shape[0]
    nlat = wlat.shape[1]
    full = lambda shape: pl.BlockSpec(shape, lambda i: (0, 0))
    kern = functools.partial(
        _mla_proj_kernel, d_model=D, q_lora=q_lora, kv_lora=kv_lora, heads=heads,
        q_scale=LOG2_E * (MLA_NOPE + MLA_ROPE) ** -0.5)
    return pl.pallas_call(
        kern,
        grid=(T // tm,),
        in_specs=[
            pl.BlockSpec((tm, D), lambda i: (i, 0)),
            full((1, D)),
            full((D, nlat)),
            full((1, q_lora)),
            full((q_lora, heads * MLA_QK_PAD)),
            full((1, kv_lora)),
            full((kv_lora, heads * MLA_NOPE)),
            full((kv_lora, heads * MLA_V)),
            pl.BlockSpec((tm, 1), lambda i: (i, 0)),
            full((1, LANES)),
        ],
        out_specs=[
            pl.BlockSpec((tm, heads * MLA_QK_PAD), lambda i: (i, 0)),
            pl.BlockSpec((tm, heads * MLA_QK_PAD), lambda i: (i, 0)),
            pl.BlockSpec((tm, heads * MLA_V), lambda i: (i, 0)),
        ],
        out_shape=[
            jax.ShapeDtypeStruct((T, heads * MLA_QK_PAD), BF16),
            jax.ShapeDtypeStruct((T, heads * MLA_QK_PAD), BF16),
            jax.ShapeDtypeStruct((T, heads * MLA_V), BF16),
        ],
        compiler_params=pltpu.CompilerParams(
            dimension_semantics=("parallel",), vmem_limit_bytes=VMEM_LIMIT),
        name="mla_proj",
    )(x2d, g.reshape(1, D), wlat, qg.reshape(1, -1), wuq, kvg.reshape(1, -1), wuk, wuv, pos2d, freq)


def _sb_attn_kernel(q_ref, k_ref, v_ref, o_ref, acc_ref, run_ref, z_ref, *, blk, hp):
    qi = pl.program_id(2)
    row = lax.broadcasted_iota(jnp.int32, (blk, blk), 0)
    col = lax.broadcasted_iota(jnp.int32, (blk, blk), 1)
    below = row > col
    suffix = below.astype(BF16)
    heads = [slice(h * SB_HEAD_DIM, (h + 1) * SB_HEAD_DIM) for h in range(hp)]

    def logits(kb, slot):
        start = pl.multiple_of(kb * blk, blk)
        for h, c in enumerate(heads):
            z_ref[slot, h] = _dot_nt(q_ref[0, :, c], k_ref[0, pl.ds(start, blk), c])

    def step(kb, slot, masked, prefetch=True):
        if prefetch:
            logits(jnp.maximum(kb - 1, 0), 1 - slot)
        start = pl.multiple_of(kb * blk, blk)
        lbs, css, tots = [], [], []
        for h in range(hp):
            z = z_ref[slot, h]
            sp = jnp.maximum(z, 0.0) + jnp.log(1.0 + jnp.exp(-jnp.abs(z)))
            if masked:
                sp = jnp.where(below, sp, 0.0)
            lbs.append(z - sp)
            cs = _dot(sp.astype(BF16), suffix)
            css.append(cs)
            tots.append(cs[:, 0:1] + sp[:, 0:1])
        for h in range(hp):
            v = v_ref[0, pl.ds(start, blk), heads[h]]
            if masked:
                w = jnp.where(below, jnp.exp(lbs[h] - css[h]), 0.0)
                acc_ref[h] = _dot(w.astype(BF16), v)
                run_ref[h] = tots[h]
            else:
                run = run_ref[h]
                w = jnp.exp(lbs[h] - css[h] - run)
                acc_ref[h] += _dot(w.astype(BF16), v)
                run_ref[h] = run + tots[h]

    logits(qi, 0)
    step(qi, 0, True)

    def pair(i, carry):
        kb = qi - 1 - 2 * i
        step(kb, 1, False)
        step(kb - 1, 0, False)
        return carry

    lax.fori_loop(0, lax.shift_right_logical(qi, 1), pair, 0)

    @pl.when(qi % 2 == 1)
    def _():
        step(0, 1, False, prefetch=False)

    for h in range(hp):
        o_ref[0, :, h * SB_HEAD_DIM:(h + 1) * SB_HEAD_DIM] = acc_ref[h].astype(o_ref.dtype)


def _sb_attn(qkv, *, heads, blk=256, hp=4):
    B, S, _ = qkv.shape
    groups = heads // hp
    wid = hp * SB_HEAD_DIM
    return pl.pallas_call(
        functools.partial(_sb_attn_kernel, blk=blk, hp=hp),
        grid=(B, groups, S // blk),
        in_specs=[
            pl.BlockSpec((1, blk, wid), lambda b, g, i: (b, i, g)),
            pl.BlockSpec((1, S, wid), lambda b, g, i: (b, 0, groups + g)),
            pl.BlockSpec((1, S, wid), lambda b, g, i: (b, 0, 2 * groups + g)),
        ],
        out_specs=pl.BlockSpec((1, blk, wid), lambda b, g, i: (b, i, g)),
        out_shape=jax.ShapeDtypeStruct((B, S, heads * SB_HEAD_DIM), BF16),
        scratch_shapes=[pltpu.VMEM((hp, blk, SB_HEAD_DIM), F32), pltpu.VMEM((hp, blk, 1), F32),
                        pltpu.VMEM((2, hp, blk, blk), F32)],
        compiler_params=pltpu.CompilerParams(
            dimension_semantics=("parallel", "parallel", "arbitrary"), vmem_limit_bytes=VMEM_LIMIT),
        name="sb_attn",
    )(qkv, qkv, qkv)


def _mla_attn_kernel(q_ref, k_ref, v_ref, o_ref, acc_ref, m_ref, z_ref, *, blk, hp):
    assert MLA_V == LANES
    qi = pl.program_id(2)
    row = lax.broadcasted_iota(jnp.int32, (blk, blk), 0)
    col = lax.broadcasted_iota(jnp.int32, (blk, blk), 1)
    visible = (col // CHUNK) <= (row // CHUNK)
    ones = jnp.ones((blk, LANES), BF16)

    def logits(kb, slot):
        start = pl.multiple_of(kb * blk, blk)
        for h in range(hp):
            cols = slice(h * MLA_QK_PAD, (h + 1) * MLA_QK_PAD)
            z_ref[slot, h] = _dot_nt(q_ref[0, :, cols], k_ref[0, pl.ds(start, blk), cols])

    def step(kb, slot, masked, prefetch=True):
        if prefetch:
            logits(jnp.maximum(kb - 1, 0), 1 - slot)
        start = pl.multiple_of(kb * blk, blk)
        for h in range(hp):
            v1 = jnp.concatenate([v_ref[0, pl.ds(start, blk), h * MLA_V:(h + 1) * MLA_V], ones], axis=1)
            z = z_ref[slot, h]
            if masked:
                z = jnp.where(visible, z, -1e30)
                m_new = jnp.broadcast_to(jnp.max(z, axis=-1, keepdims=True), (blk, LANES))
                p = jnp.exp2(z - jnp.concatenate([m_new, m_new], axis=1))
                acc_ref[h] = _dot(p.astype(BF16), v1)
            else:
                m = m_ref[h]
                m_new = jnp.maximum(m, jnp.max(z, axis=-1, keepdims=True))
                p = jnp.exp2(z - jnp.concatenate([m_new, m_new], axis=1))
                alpha = jnp.exp2(m - m_new)
                acc_ref[h] = jnp.concatenate([alpha, alpha], axis=1) * acc_ref[h] + _dot(p.astype(BF16), v1)
            m_ref[h] = m_new

    logits(qi, 0)
    step(qi, 0, True)

    def pair(i, carry):
        kb = qi - 1 - 2 * i
        step(kb, 1, False)
        step(kb - 1, 0, False)
        return carry

    lax.fori_loop(0, lax.shift_right_logical(qi, 1), pair, 0)

    @pl.when(qi % 2 == 1)
    def _():
        step(0, 1, False, prefetch=False)

    for h in range(hp):
        o_ref[0, :, h * MLA_V:(h + 1) * MLA_V] = (
            acc_ref[h, :, :MLA_V] / acc_ref[h, :, MLA_V:]).astype(o_ref.dtype)


def _mla_attn(q, k, v, *, heads, blk=256, hp=2):
    B, S, _ = q.shape
    return pl.pallas_call(
        functools.partial(_mla_attn_kernel, blk=blk, hp=hp),
        grid=(B, heads // hp, S // blk),
        in_specs=[
            pl.BlockSpec((1, blk, hp * MLA_QK_PAD), lambda b, g, i: (b, i, g)),
            pl.BlockSpec((1, S, hp * MLA_QK_PAD), lambda b, g, i: (b, 0, g)),
            pl.BlockSpec((1, S, hp * MLA_V), lambda b, g, i: (b, 0, g)),
        ],
        out_specs=pl.BlockSpec((1, blk, hp * MLA_V), lambda b, g, i: (b, i, g)),
        out_shape=jax.ShapeDtypeStruct((B, S, heads * MLA_V), BF16),
        scratch_shapes=[pltpu.VMEM((hp, blk, 2 * MLA_V), F32), pltpu.VMEM((hp, blk, LANES), F32),
                        pltpu.VMEM((2, hp, blk, blk), F32)],
        compiler_params=pltpu.CompilerParams(
            dimension_semantics=("parallel", "parallel", "arbitrary"), vmem_limit_bytes=VMEM_LIMIT),
        name="mla_attn",
    )(q, k, v)


def _mem_attn_kernel(q_ref, kv_ref, o_ref, *, heads, head_dim, d_model):
    for hd in range(heads):
        q = q_ref[0, :, hd * head_dim:(hd + 1) * head_dim]
        k = kv_ref[0, :, hd * head_dim:(hd + 1) * head_dim]
        v = kv_ref[0, :, d_model + hd * head_dim:d_model + (hd + 1) * head_dim]
        z = _dot_nt(q, k)
        p = jnp.exp(z - jnp.max(z, axis=-1, keepdims=True))
        l = jnp.sum(p, axis=-1, keepdims=True)
        o = _dot(p.astype(BF16), v) / l
        o_ref[0, :, hd * head_dim:(hd + 1) * head_dim] = o.astype(o_ref.dtype)


def _mem_attn(qm, kvm, *, heads, tq=512):
    B, S, D = qm.shape
    M = kvm.shape[1]
    return pl.pallas_call(
        functools.partial(_mem_attn_kernel, heads=heads, head_dim=D // heads, d_model=D),
        grid=(B, S // tq),
        in_specs=[
            pl.BlockSpec((1, tq, D), lambda b, i: (b, i, 0)),
            pl.BlockSpec((1, M, 2 * D), lambda b, i: (b, 0, 0)),
        ],
        out_specs=pl.BlockSpec((1, tq, D), lambda b, i: (b, i, 0)),
        out_shape=jax.ShapeDtypeStruct((B, S, D), BF16),
        compiler_params=pltpu.CompilerParams(
            dimension_semantics=("parallel", "parallel"), vmem_limit_bytes=VMEM_LIMIT),
        name="mem_attn",
    )(qm, kvm)


def _ffn_kernel(x_ref, g_ref, w1_ref, w2_ref, gf_ref, o_ref, h_ref, acc_ref, *, d_model):
    f = pl.program_id(1)

    @pl.when(f == 0)
    def _():
        xf = x_ref[...]
        h_ref[...] = (xf * _rms_scale(xf, d_model) * g_ref[...]).astype(BF16)
        acc_ref[...] = jnp.zeros_like(acc_ref)

    a = jnp.maximum(_dot(h_ref[...], w1_ref[...]), 0.0)
    acc_ref[...] += _dot((a * a).astype(BF16), w2_ref[...])

    @pl.when(f == pl.num_programs(1) - 1)
    def _():
        y = x_ref[...] + acc_ref[...]
        o_ref[...] = y * _rms_scale(y, d_model) * gf_ref[...]


def _ffn(x2d, g, w1, w2, gf, *, tm=512, tf=1024):
    T, D = x2d.shape
    F = w1.shape[1]
    return pl.pallas_call(
        functools.partial(_ffn_kernel, d_model=D),
        grid=(T // tm, F // tf),
        in_specs=[
            pl.BlockSpec((tm, D), lambda i, f: (i, 0)),
            pl.BlockSpec((1, D), lambda i, f: (0, 0)),
            pl.BlockSpec((D, tf), lambda i, f: (0, f)),
            pl.BlockSpec((tf, D), lambda i, f: (f, 0)),
            pl.BlockSpec((1, D), lambda i, f: (0, 0)),
        ],
        out_specs=pl.BlockSpec((tm, D), lambda i, f: (i, 0)),
        out_shape=jax.ShapeDtypeStruct((T, D), F32),
        scratch_shapes=[pltpu.VMEM((tm, D), BF16), pltpu.VMEM((tm, D), F32)],
        compiler_params=pltpu.CompilerParams(
            dimension_semantics=("parallel", "arbitrary"), vmem_limit_bytes=VMEM_LIMIT),
        name="ffn",
    )(x2d, g.reshape(1, D), w1, w2, gf.reshape(1, D))


def _pad_cols(w, n):
    return jnp.pad(w, ((0, 0), (0, n - w.shape[1])))


def kernel(x, mem, positions, norm_mix_g, w_in, q_norm_g, w_uq, kv_norm_g, w_ukv, gn_sb_g, gn_mla_g, w_out, norm_mem_g, mem_kv_norm_g, w_mq, w_mkv, w_mo, norm_ffn_g, w_ff1, w_ff2, final_norm_g):
    B, S, D = x.shape
    M = mem.shape[1]
    T = B * S
    depth = w_in.shape[0]
    sb_width = gn_sb_g.shape[1]
    sb_heads = sb_width // SB_HEAD_DIM
    mla_heads = gn_mla_g.shape[1] // MLA_V
    q_lora = q_norm_g.shape[1]
    kv_lora = kv_norm_g.shape[1]
    mem_head_dim = D // MEM_HEADS

    half = MLA_ROPE // 2
    inv_freq = ROPE_THETA ** (-jnp.arange(half, dtype=F32) / half)
    freq = jnp.concatenate([inv_freq, inv_freq, jnp.zeros((LANES - 2 * half,), F32)]).reshape(1, LANES)
    pos2d = positions.reshape(T, 1)

    xc = x.reshape(T, D)
    for l in range(depth):
        w_sb = w_in[l][:, :3 * sb_width].astype(BF16)
        w_lat = _pad_cols(w_in[l][:, 3 * sb_width:], q_lora + kv_lora + LANES).astype(BF16)
        wq = w_uq[l].reshape(q_lora, mla_heads, MLA_NOPE + MLA_ROPE)
        wq = jnp.pad(wq, ((0, 0), (0, 0), (0, MLA_QK_PAD - MLA_NOPE - MLA_ROPE)))
        wq = wq.reshape(q_lora, mla_heads * MLA_QK_PAD).astype(BF16)
        wkv = w_ukv[l].reshape(kv_lora, mla_heads, MLA_NOPE + MLA_V)
        wuk = wkv[:, :, :MLA_NOPE].reshape(kv_lora, mla_heads * MLA_NOPE).astype(BF16)
        wuv = wkv[:, :, MLA_NOPE:].reshape(kv_lora, mla_heads * MLA_V).astype(BF16)
        sb_scale = jnp.concatenate([
            jnp.full((sb_width,), SB_HEAD_DIM ** -0.5, F32), jnp.ones((2 * sb_width,), F32)])
        mq_scale = jnp.full((D,), mem_head_dim ** -0.5, F32)

        qkv = _norm_matmul([xc], [norm_mix_g[l]], w_sb, name="sb_qkv_proj", col_scale=sb_scale)
        q_mla, k_mla, v_mla = _mla_proj(
            xc, norm_mix_g[l], w_lat, q_norm_g[l], wq, kv_norm_g[l], wuk, wuv, pos2d, freq,
            heads=mla_heads)
        o_sb = _sb_attn(qkv.reshape(B, S, 3 * sb_width), heads=sb_heads)
        o_mla = _mla_attn(q_mla.reshape(B, S, -1), k_mla.reshape(B, S, -1), v_mla.reshape(B, S, -1),
                          heads=mla_heads)
        xc = _norm_matmul(
            [o_sb.reshape(T, -1), o_mla.reshape(T, -1)], [gn_sb_g[l], gn_mla_g[l]],
            w_out[l].astype(BF16), name="out_proj", residual=xc, out_dtype=F32)

        qm = _norm_matmul([xc], [norm_mem_g[l]], w_mq[l].astype(BF16), name="mem_q_proj",
                          col_scale=mq_scale)
        kvm = _norm_matmul([mem.reshape(B * M, D)], [mem_kv_norm_g[l]], w_mkv[l].astype(BF16),
                           name="mem_kv_proj")
        om = _mem_attn(qm.reshape(B, S, D), kvm.reshape(B, M, 2 * D), heads=MEM_HEADS)
        xc = _norm_matmul([om.reshape(T, D)], None, w_mo[l].astype(BF16), name="mem_o_proj",
                          residual=xc, out_dtype=F32)

        assert depth == 1
        xc = _ffn(xc, norm_ffn_g[l], w_ff1[l].astype(BF16), w_ff2[l].astype(BF16), final_norm_g)
    return xc.reshape(B, S, D)
```

```python
import functools

import jax
import jax.numpy as jnp
from jax import lax
from jax.experimental import pallas as pl
from jax.experimental.pallas import tpu as pltpu

F32 = jnp.float32
BF16 = jnp.bfloat16

EPS = 1e-6
CHUNK = 64
SB_HEAD_DIM = 128
MLA_NOPE = 128
MLA_ROPE = 64
MLA_V = 128
MLA_QK_PAD = 256
ROPE_THETA = 10000.0
LOG2_E = 1.4426950408889634
MEM_HEADS = 4
LANES = 128
VMEM_LIMIT = 56 * 1024 * 1024


def _rms_scale(xf, width):
    return lax.rsqrt(jnp.sum(xf * xf, axis=-1, keepdims=True) * (1.0 / width) + EPS)


def _dot(a, b):
    return jnp.dot(a, b, preferred_element_type=F32)


def _dot_nt(a, b):
    return lax.dot_general(a, b, (((1,), (1,)), ((), ())), preferred_element_type=F32)


def _norm_matmul_kernel(*refs, n_parts, widths, normed, has_scale, has_res, tn):
    pos = 0
    x_refs = refs[pos:pos + n_parts]; pos += n_parts
    g_refs = ()
    if normed:
        g_refs = refs[pos:pos + n_parts]; pos += n_parts
    w_ref = refs[pos]; pos += 1
    scale_ref = None
    if has_scale:
        scale_ref = refs[pos]; pos += 1
    res_ref = None
    if has_res:
        res_ref = refs[pos]; pos += 1
    o_ref = refs[pos]

    hs = []
    for p in range(n_parts):
        xf = x_refs[p][...].astype(F32)
        if normed:
            xf = xf * _rms_scale(xf, widths[p]) * g_refs[p][...]
        hs.append(xf.astype(BF16))
    h = hs[0] if n_parts == 1 else jnp.concatenate(hs, axis=1)
    for c in range(0, o_ref.shape[1], tn):
        acc = _dot(h, w_ref[:, c:c + tn])
        if has_scale:
            acc = acc * scale_ref[:, c:c + tn]
        if has_res:
            acc = acc + res_ref[:, c:c + tn]
        o_ref[:, c:c + tn] = acc.astype(o_ref.dtype)


def _norm_matmul(parts, gains, w, *, name, col_scale=None, residual=None, out_dtype=BF16, tm=512, tn=512):
    T = parts[0].shape[0]
    widths = tuple(p.shape[1] for p in parts)
    K = sum(widths)
    N = w.shape[1]
    tm = min(tm, T)
    tn = min(tn, N)
    assert T % tm == 0 and N % tn == 0 and w.shape[0] == K
    normed = gains is not None
    in_specs = [pl.BlockSpec((tm, k), lambda i: (i, 0)) for k in widths]
    args = list(parts)
    if normed:
        in_specs += [pl.BlockSpec((1, k), lambda i: (0, 0)) for k in widths]
        args += [g.reshape(1, -1).astype(F32) for g in gains]
    in_specs.append(pl.BlockSpec((K, N), lambda i: (0, 0), pipeline_mode=pl.Buffered(1)))
    args.append(w)
    if col_scale is not None:
        in_specs.append(pl.BlockSpec((1, N), lambda i: (0, 0)))
        args.append(col_scale.reshape(1, N).astype(F32))
    if residual is not None:
        in_specs.append(pl.BlockSpec((tm, N), lambda i: (i, 0)))
        args.append(residual)
    kern = functools.partial(
        _norm_matmul_kernel, n_parts=len(parts), widths=widths, normed=normed,
        has_scale=col_scale is not None, has_res=residual is not None, tn=tn)
    return pl.pallas_call(
        kern,
        grid=(T // tm,),
        in_specs=in_specs,
        out_specs=pl.BlockSpec((tm, N), lambda i: (i, 0)),
        out_shape=jax.ShapeDtypeStruct((T, N), out_dtype),
        compiler_params=pltpu.CompilerParams(
            dimension_semantics=("parallel",), vmem_limit_bytes=VMEM_LIMIT),
        name=name,
    )(*args)


def _rope(xp, cos_t, sin_lo, sin_hi):
    half = MLA_ROPE // 2
    return xp * cos_t + pltpu.roll(xp, LANES - half, 1) * sin_lo + pltpu.roll(xp, half, 1) * sin_hi


def _mla_proj_kernel(x_ref, g_ref, wlat_ref, qg_ref, wuq_ref, kvg_ref, wuk_ref, wuv_ref,
                     pos_ref, freq_ref, q_ref, k_ref, v_ref, *, d_model, q_lora, kv_lora, heads, q_scale):
    xf = x_ref[...]
    h = (xf * _rms_scale(xf, d_model) * g_ref[...]).astype(BF16)
    lat = _dot(h, wlat_ref[...])
    cq = lat[:, :q_lora]
    ckv = lat[:, q_lora:q_lora + kv_lora]
    kpe = lat[:, q_lora + kv_lora:]

    half = MLA_ROPE // 2
    ang = pos_ref[...].astype(F32) * freq_ref[...]
    cos_t = jnp.cos(ang)
    sin_t = jnp.sin(ang)
    lane = lax.broadcasted_iota(jnp.int32, ang.shape, 1)
    sin_lo = jnp.where(lane < half, -sin_t, 0.0)
    sin_hi = jnp.where((lane >= half) & (lane < 2 * half), sin_t, 0.0)

    cqn = (cq * _rms_scale(cq, q_lora) * qg_ref[...]).astype(BF16)
    q = _dot(cqn, wuq_ref[...]) * q_scale
    ckvn = (ckv * _rms_scale(ckv, kv_lora) * kvg_ref[...]).astype(BF16)
    kn = _dot(ckvn, wuk_ref[...])
    v_ref[...] = _dot(ckvn, wuv_ref[...]).astype(v_ref.dtype)
    kpe_r = _rope(kpe, cos_t, sin_lo, sin_hi).astype(k_ref.dtype)
    for hd in range(heads):
        base = hd * MLA_QK_PAD
        q_ref[:, base:base + MLA_NOPE] = q[:, base:base + MLA_NOPE].astype(q_ref.dtype)
        q_ref[:, base + MLA_NOPE:base + MLA_QK_PAD] = _rope(
            q[:, base + MLA_NOPE:base + MLA_QK_PAD], cos_t, sin_lo, sin_hi).astype(q_ref.dtype)
        k_ref[:, base:base + MLA_NOPE] = kn[:, hd * MLA_NOPE:(hd + 1) * MLA_NOPE].astype(k_ref.dtype)
        k_ref[:, base + MLA_NOPE:base + MLA_QK_PAD] = kpe_r


def _mla_proj(x2d, g, wlat, qg, wuq, kvg, wuk, wuv, pos2d, freq, *, heads, tm=512):
    T, D = x2d.shape
    q_lora = qg.shape[0]
    kv_lora = kvg.shape[0]
    nlat = wlat.shape[1]
    full = lambda shape: pl.BlockSpec(shape, lambda i: (0, 0))
    kern = functools.partial(
        _mla_proj_kernel, d_model=D, q_lora=q_lora, kv_lora=kv_lora, heads=heads,
        q_scale=LOG2_E * (MLA_NOPE + MLA_ROPE) ** -0.5)
    return pl.pallas_call(
        kern,
        grid=(T // tm,),
        in_specs=[
            pl.BlockSpec((tm, D), lambda i: (i, 0)),
            full((1, D)),
            full((D, nlat)),
            full((1, q_lora)),
            full((q_lora, heads * MLA_QK_PAD)),
            full((1, kv_lora)),
            full((kv_lora, heads * MLA_NOPE)),
            full((kv_lora, heads * MLA_V)),
            pl.BlockSpec((tm, 1), lambda i: (i, 0)),
            full((1, LANES)),
        ],
        out_specs=[
            pl.BlockSpec((tm, heads * MLA_QK_PAD), lambda i: (i, 0)),
            pl.BlockSpec((tm, heads * MLA_QK_PAD), lambda i: (i, 0)),
            pl.BlockSpec((tm, heads * MLA_V), lambda i: (i, 0)),
        ],
        out_shape=[
            jax.ShapeDtypeStruct((T, heads * MLA_QK_PAD), BF16),
            jax.ShapeDtypeStruct((T, heads * MLA_QK_PAD), BF16),
            jax.ShapeDtypeStruct((T, heads * MLA_V), BF16),
        ],
        compiler_params=pltpu.CompilerParams(
            dimension_semantics=("parallel",), vmem_limit_bytes=VMEM_LIMIT),
        name="mla_proj",
    )(x2d, g.reshape(1, D), wlat, qg.reshape(1, -1), wuq, kvg.reshape(1, -1), wuk, wuv, pos2d, freq)


def _walk_key_blocks(qi, step):
    step(qi, 0, True, True)

    def quad(i, carry):
        kb = qi - 1 - 4 * i
        step(kb, 1, False, True)
        step(kb - 1, 0, False, True)
        step(kb - 2, 1, False, True)
        step(kb - 3, 0, False, True)
        return carry

    quads = lax.shift_right_logical(qi, 2)
    lax.fori_loop(0, quads, quad, 0)
    rem = qi - 4 * quads

    @pl.when(rem >= 2)
    def _():
        step(rem - 1, 1, False, True)
        step(rem - 2, 0, False, True)

    @pl.when(rem % 2 == 1)
    def _():
        step(0, 1, False, False)


def _sb_attn_kernel(q_ref, k_ref, v_ref, o_ref, acc_ref, run_ref, z_ref, *, blk, hp):
    qi = pl.program_id(2)
    row = lax.broadcasted_iota(jnp.int32, (blk, blk), 0)
    col = lax.broadcasted_iota(jnp.int32, (blk, blk), 1)
    below = row > col
    suffix = below.astype(BF16)
    heads = [slice(h * SB_HEAD_DIM, (h + 1) * SB_HEAD_DIM) for h in range(hp)]

    def logits(kb, slot):
        start = pl.multiple_of(kb * blk, blk)
        for h, c in enumerate(heads):
            z_ref[slot, h] = _dot_nt(q_ref[0, :, c], k_ref[0, pl.ds(start, blk), c])

    def step(kb, slot, masked, prefetch=True):
        if prefetch:
            logits(jnp.maximum(kb - 1, 0), 1 - slot)
        start = pl.multiple_of(kb * blk, blk)
        lbs, sums = [], []
        for h in range(hp):
            z = z_ref[slot, h]
            sp = jnp.maximum(z, 0.0) + jnp.log(1.0 + jnp.exp(-jnp.abs(z)))
            if masked:
                sp = jnp.where(below, sp, 0.0)
            lbs.append(z - sp)
            cs = _dot(sp.astype(BF16), suffix)
            sums.append((cs, cs[:, 0:1] + sp[:, 0:1]))
        for h in range(hp):
            v = v_ref[0, pl.ds(start, blk), heads[h]]
            cs, tot = sums[h]
            if masked:
                w = jnp.where(below, jnp.exp(lbs[h] - cs), 0.0)
                acc_ref[h] = _dot(w.astype(BF16), v)
                run_ref[h] = tot
            else:
                run = run_ref[h]
                w = jnp.exp(lbs[h] - cs - run)
                acc_ref[h] += _dot(w.astype(BF16), v)
                run_ref[h] = run + tot

    logits(qi, 0)
    _walk_key_blocks(qi, step)
    for h in range(hp):
        o_ref[0, :, h * SB_HEAD_DIM:(h + 1) * SB_HEAD_DIM] = acc_ref[h].astype(o_ref.dtype)


def _sb_attn(qkv, *, heads, blk=256, hp=4):
    B, S, _ = qkv.shape
    groups = heads // hp
    wid = hp * SB_HEAD_DIM
    return pl.pallas_call(
        functools.partial(_sb_attn_kernel, blk=blk, hp=hp),
        grid=(B, groups, S // blk),
        in_specs=[
            pl.BlockSpec((1, blk, wid), lambda b, g, i: (b, i, g)),
            pl.BlockSpec((1, S, wid), lambda b, g, i: (b, 0, groups + g)),
            pl.BlockSpec((1, S, wid), lambda b, g, i: (b, 0, 2 * groups + g)),
        ],
        out_specs=pl.BlockSpec((1, blk, wid), lambda b, g, i: (b, i, g)),
        out_shape=jax.ShapeDtypeStruct((B, S, heads * SB_HEAD_DIM), BF16),
        scratch_shapes=[pltpu.VMEM((hp, blk, SB_HEAD_DIM), F32), pltpu.VMEM((hp, blk, 1), F32),
                        pltpu.VMEM((2, hp, blk, blk), F32)],
        compiler_params=pltpu.CompilerParams(
            dimension_semantics=("parallel", "parallel", "arbitrary"), vmem_limit_bytes=VMEM_LIMIT),
        name="sb_attn",
    )(qkv, qkv, qkv)


def _mla_attn_kernel(q_ref, k_ref, v_ref, o_ref, acc_ref, m_ref, z_ref, *, blk, hp):
    assert MLA_V == LANES
    qi = pl.program_id(2)
    row = lax.broadcasted_iota(jnp.int32, (blk, blk), 0)
    col = lax.broadcasted_iota(jnp.int32, (blk, blk), 1)
    visible = (col // CHUNK) <= (row // CHUNK)
    ones = jnp.ones((blk, LANES), BF16)

    def logits(kb, slot):
        start = pl.multiple_of(kb * blk, blk)
        for h in range(hp):
            cols = slice(h * MLA_QK_PAD, (h + 1) * MLA_QK_PAD)
            z_ref[slot, h] = _dot_nt(q_ref[0, :, cols], k_ref[0, pl.ds(start, blk), cols])

    def logits2(kb, pair_slot):
        logits(jnp.maximum(kb, 0), 2 * pair_slot)
        logits(jnp.maximum(kb - 1, 0), 2 * pair_slot + 1)

    def v_ones(kb, h):
        start = pl.multiple_of(kb * blk, blk)
        return jnp.concatenate([v_ref[0, pl.ds(start, blk), h * MLA_V:(h + 1) * MLA_V], ones], axis=1)

    def update(h, zs, v1s, first):
        zmax = zs[0] if len(zs) == 1 else jnp.maximum(zs[0], zs[1])
        if first:
            m_new = jnp.broadcast_to(jnp.max(zmax, axis=-1, keepdims=True), (blk, LANES))
        else:
            m = m_ref[h]
            m_new = jnp.maximum(m, jnp.max(zmax, axis=-1, keepdims=True))
        mm = jnp.concatenate([m_new, m_new], axis=1)
        p = jnp.concatenate([jnp.exp2(z - mm).astype(BF16) for z in zs], axis=1)
        pv = _dot(p, v1s[0] if len(zs) == 1 else jnp.concatenate(v1s, axis=0))
        if first:
            acc_ref[h] = pv
        else:
            alpha = jnp.exp2(m - m_new)
            acc_ref[h] = jnp.concatenate([alpha, alpha], axis=1) * acc_ref[h] + pv
        m_ref[h] = m_new

    def double(kb, pair_slot, prefetch=True):
        if prefetch:
            logits2(kb - 2, 1 - pair_slot)
        for h in range(hp):
            update(h, [z_ref[2 * pair_slot, h], z_ref[2 * pair_slot + 1, h]],
                   [v_ones(kb, h), v_ones(kb - 1, h)], False)

    def single(slot):
        for h in range(hp):
            update(h, [z_ref[slot, h]], [v_ones(0, h)], False)

    logits(qi, 0)
    logits2(qi - 1, 1)
    for h in range(hp):
        update(h, [jnp.where(visible, z_ref[0, h], -1e30)], [v_ones(qi, h)], True)

    def trip(i, carry):
        kb = qi - 1 - 4 * i
        double(kb, 1)
        double(kb - 2, 0)
        return carry

    pairs = lax.shift_right_logical(qi, 1)
    trips = lax.shift_right_logical(pairs, 1)
    lax.fori_loop(0, trips, trip, 0)
    odd_pair = pairs % 2 == 1
    odd_block = qi % 2 == 1

    @pl.when(odd_pair)
    def _():
        double(qi - 1 - 4 * trips, 1)

    @pl.when(odd_block & odd_pair)
    def _():
        single(0)

    @pl.when(odd_block & jnp.logical_not(odd_pair))
    def _():
        single(2)

    for h in range(hp):
        o_ref[0, :, h * MLA_V:(h + 1) * MLA_V] = (
            acc_ref[h, :, :MLA_V] / acc_ref[h, :, MLA_V:]).astype(o_ref.dtype)


def _mla_attn(q, k, v, *, heads, blk=256, hp=4):
    B, S, _ = q.shape
    return pl.pallas_call(
        functools.partial(_mla_attn_kernel, blk=blk, hp=hp),
        grid=(B, heads // hp, S // blk),
        in_specs=[
            pl.BlockSpec((1, blk, hp * MLA_QK_PAD), lambda b, g, i: (b, i, g)),
            pl.BlockSpec((1, S, hp * MLA_QK_PAD), lambda b, g, i: (b, 0, g)),
            pl.BlockSpec((1, S, hp * MLA_V), lambda b, g, i: (b, 0, g)),
        ],
        out_specs=pl.BlockSpec((1, blk, hp * MLA_V), lambda b, g, i: (b, i, g)),
        out_shape=jax.ShapeDtypeStruct((B, S, heads * MLA_V), BF16),
        scratch_shapes=[pltpu.VMEM((hp, blk, 2 * MLA_V), F32), pltpu.VMEM((hp, blk, LANES), F32),
                        pltpu.VMEM((4, hp, blk, blk), F32)],
        compiler_params=pltpu.CompilerParams(
            dimension_semantics=("parallel", "parallel", "arbitrary"), vmem_limit_bytes=VMEM_LIMIT),
        name="mla_attn",
    )(q, k, v)


def _mem_attn_kernel(q_ref, kv_ref, o_ref, *, heads, head_dim, d_model):
    for hd in range(heads):
        q = q_ref[0, :, hd * head_dim:(hd + 1) * head_dim]
        k = kv_ref[0, :, hd * head_dim:(hd + 1) * head_dim]
        v = kv_ref[0, :, d_model + hd * head_dim:d_model + (hd + 1) * head_dim]
        z = _dot_nt(q, k)
        p = jnp.exp(z - jnp.max(z, axis=-1, keepdims=True))
        l = jnp.sum(p, axis=-1, keepdims=True)
        o = _dot(p.astype(BF16), v) / l
        o_ref[0, :, hd * head_dim:(hd + 1) * head_dim] = o.astype(o_ref.dtype)


def _mem_attn(qm, kvm, *, heads, tq=512):
    B, S, D = qm.shape
    M = kvm.shape[1]
    return pl.pallas_call(
        functools.partial(_mem_attn_kernel, heads=heads, head_dim=D // heads, d_model=D),
        grid=(B, S // tq),
        in_specs=[
            pl.BlockSpec((1, tq, D), lambda b, i: (b, i, 0)),
            pl.BlockSpec((1, M, 2 * D), lambda b, i: (b, 0, 0)),
        ],
        out_specs=pl.BlockSpec((1, tq, D), lambda b, i: (b, i, 0)),
        out_shape=jax.ShapeDtypeStruct((B, S, D), BF16),
        compiler_params=pltpu.CompilerParams(
            dimension_semantics=("parallel", "parallel"), vmem_limit_bytes=VMEM_LIMIT),
        name="mem_attn",
    )(qm, kvm)


def _ffn_kernel(x_ref, g_ref, w1_ref, w2_ref, gf_ref, o_ref, h_ref, acc_ref, *, d_model):
    f = pl.program_id(1)

    @pl.when(f == 0)
    def _():
        xf = x_ref[...]
        h_ref[...] = (xf * _rms_scale(xf, d_model) * g_ref[...]).astype(BF16)
        acc_ref[...] = jnp.zeros_like(acc_ref)

    a = jnp.maximum(_dot(h_ref[...], w1_ref[...]), 0.0)
    acc_ref[...] += _dot((a * a).astype(BF16), w2_ref[...])

    @pl.when(f == pl.num_programs(1) - 1)
    def _():
        y = x_ref[...] + acc_ref[...]
        o_ref[...] = y * _rms_scale(y, d_model) * gf_ref[...]


def _ffn(x2d, g, w1, w2, gf, *, tm=512, tf=1024):
    T, D = x2d.shape
    F = w1.shape[1]
    return pl.pallas_call(
        functools.partial(_ffn_kernel, d_model=D),
        grid=(T // tm, F // tf),
        in_specs=[
            pl.BlockSpec((tm, D), lambda i, f: (i, 0)),
            pl.BlockSpec((1, D), lambda i, f: (0, 0)),
            pl.BlockSpec((D, tf), lambda i, f: (0, f)),
            pl.BlockSpec((tf, D), lambda i, f: (f, 0)),
            pl.BlockSpec((1, D), lambda i, f: (0, 0)),
        ],
        out_specs=pl.BlockSpec((tm, D), lambda i, f: (i, 0)),
        out_shape=jax.ShapeDtypeStruct((T, D), F32),
        scratch_shapes=[pltpu.VMEM((tm, D), BF16), pltpu.VMEM((tm, D), F32)],
        compiler_params=pltpu.CompilerParams(
            dimension_semantics=("parallel", "arbitrary"), vmem_limit_bytes=VMEM_LIMIT),
        name="ffn",
    )(x2d, g.reshape(1, D), w1, w2, gf.reshape(1, D))


def _pad_cols(w, n):
    return jnp.pad(w, ((0, 0), (0, n - w.shape[1])))


def kernel(x, mem, positions, norm_mix_g, w_in, q_norm_g, w_uq, kv_norm_g, w_ukv, gn_sb_g, gn_mla_g, w_out, norm_mem_g, mem_kv_norm_g, w_mq, w_mkv, w_mo, norm_ffn_g, w_ff1, w_ff2, final_norm_g):
    B, S, D = x.shape
    M = mem.shape[1]
    T = B * S
    depth = w_in.shape[0]
    sb_width = gn_sb_g.shape[1]
    sb_heads = sb_width // SB_HEAD_DIM
    mla_heads = gn_mla_g.shape[1] // MLA_V
    q_lora = q_norm_g.shape[1]
    kv_lora = kv_norm_g.shape[1]
    mem_head_dim = D // MEM_HEADS

    half = MLA_ROPE // 2
    inv_freq = ROPE_THETA ** (-jnp.arange(half, dtype=F32) / half)
    freq = jnp.concatenate([inv_freq, inv_freq, jnp.zeros((LANES - 2 * half,), F32)]).reshape(1, LANES)
    pos2d = positions.reshape(T, 1)

    xc = x.reshape(T, D)
    for l in range(depth):
        w_sb = w_in[l][:, :3 * sb_width].astype(BF16)
        w_lat = _pad_cols(w_in[l][:, 3 * sb_width:], q_lora + kv_lora + LANES).astype(BF16)
        wq = w_uq[l].reshape(q_lora, mla_heads, MLA_NOPE + MLA_ROPE)
        wq = jnp.pad(wq, ((0, 0), (0, 0), (0, MLA_QK_PAD - MLA_NOPE - MLA_ROPE)))
        wq = wq.reshape(q_lora, mla_heads * MLA_QK_PAD).astype(BF16)
        wkv = w_ukv[l].reshape(kv_lora, mla_heads, MLA_NOPE + MLA_V)
        wuk = wkv[:, :, :MLA_NOPE].reshape(kv_lora, mla_heads * MLA_NOPE).astype(BF16)
        wuv = wkv[:, :, MLA_NOPE:].reshape(kv_lora, mla_heads * MLA_V).astype(BF16)
        sb_scale = jnp.concatenate([
            jnp.full((sb_width,), SB_HEAD_DIM ** -0.5, F32), jnp.ones((2 * sb_width,), F32)])
        mq_scale = jnp.full((D,), mem_head_dim ** -0.5, F32)

        qkv = _norm_matmul([xc], [norm_mix_g[l]], w_sb, name="sb_qkv_proj", col_scale=sb_scale)
        q_mla, k_mla, v_mla = _mla_proj(
            xc, norm_mix_g[l], w_lat, q_norm_g[l], wq, kv_norm_g[l], wuk, wuv, pos2d, freq,
            heads=mla_heads)
        o_sb = _sb_attn(qkv.reshape(B, S, 3 * sb_width), heads=sb_heads)
        o_mla = _mla_attn(q_mla.reshape(B, S, -1), k_mla.reshape(B, S, -1), v_mla.reshape(B, S, -1),
                          heads=mla_heads)
        xc = _norm_matmul(
            [o_sb.reshape(T, -1), o_mla.reshape(T, -1)], [gn_sb_g[l], gn_mla_g[l]],
            w_out[l].astype(BF16), name="out_proj", residual=xc, out_dtype=F32)

        qm = _norm_matmul([xc], [norm_mem_g[l]], w_mq[l].astype(BF16), name="mem_q_proj",
                          col_scale=mq_scale)
        kvm = _norm_matmul([mem.reshape(B * M, D)], [mem_kv_norm_g[l]], w_mkv[l].astype(BF16),
                           name="mem_kv_proj")
        om = _mem_attn(qm.reshape(B, S, D), kvm.reshape(B, M, 2 * D), heads=MEM_HEADS)
        xc = _norm_matmul([om.reshape(T, D)], None, w_mo[l].astype(BF16), name="mem_o_proj",
                          residual=xc, out_dtype=F32)

        assert depth == 1
        xc = _ffn(xc, norm_ffn_g[l], w_ff1[l].astype(BF16), w_ff2[l].astype(BF16), final_norm_g)
    return xc.reshape(B, S, D)
```

```python
import functools

import jax
import jax.numpy as jnp
from jax import lax
from jax.experimental import pallas as pl
from jax.experimental.pallas import tpu as pltpu

F32 = jnp.float32
BF16 = jnp.bfloat16

EPS = 1e-6
CHUNK = 64
SB_HEAD_DIM = 128
MLA_NOPE = 128
MLA_ROPE = 64
MLA_V = 128
MLA_QK_PAD = 256
ROPE_THETA = 10000.0
LOG2_E = 1.4426950408889634
MEM_HEADS = 4
LANES = 128
VMEM_LIMIT = 56 * 1024 * 1024


def _rms_scale(xf, width):
    return lax.rsqrt(jnp.sum(xf * xf, axis=-1, keepdims=True) * (1.0 / width) + EPS)


def _dot(a, b):
    return jnp.dot(a, b, preferred_element_type=F32)


def _dot_nt(a, b):
    return lax.dot_general(a, b, (((1,), (1,)), ((), ())), preferred_element_type=F32)


def _norm_matmul_kernel(*refs, n_parts, widths, normed, has_scale, has_res, tn):
    pos = 0
    x_refs = refs[pos:pos + n_parts]; pos += n_parts
    g_refs = ()
    if normed:
        g_refs = refs[pos:pos + n_parts]; pos += n_parts
    w_ref = refs[pos]; pos += 1
    scale_ref = None
    if has_scale:
        scale_ref = refs[pos]; pos += 1
    res_ref = None
    if has_res:
        res_ref = refs[pos]; pos += 1
    o_ref = refs[pos]

    hs = []
    for p in range(n_parts):
        xf = x_refs[p][...].astype(F32)
        if normed:
            xf = xf * _rms_scale(xf, widths[p]) * g_refs[p][...]
        hs.append(xf.astype(BF16))
    h = hs[0] if n_parts == 1 else jnp.concatenate(hs, axis=1)
    for c in range(0, o_ref.shape[1], tn):
        acc = _dot(h, w_ref[:, c:c + tn])
        if has_scale:
            acc = acc * scale_ref[:, c:c + tn]
        if has_res:
            acc = acc + res_ref[:, c:c + tn]
        o_ref[:, c:c + tn] = acc.astype(o_ref.dtype)


def _norm_matmul(parts, gains, w, *, name, n_out=None, col_scale=None, residual=None, out_dtype=BF16,
                 tm=512, tn=512):
    T = parts[0].shape[0]
    widths = tuple(p.shape[1] for p in parts)
    K = sum(widths)
    N = w.shape[1] if n_out is None else n_out
    tm = min(tm, T)
    tn = min(tn, N)
    assert T % tm == 0 and N % tn == 0 and w.shape[0] == K
    normed = gains is not None
    in_specs = [pl.BlockSpec((tm, k), lambda i: (i, 0)) for k in widths]
    args = list(parts)
    if normed:
        in_specs += [pl.BlockSpec((1, k), lambda i: (0, 0)) for k in widths]
        args += [g.reshape(1, -1).astype(F32) for g in gains]
    in_specs.append(pl.BlockSpec((K, N), lambda i: (0, 0), pipeline_mode=pl.Buffered(1)))
    args.append(w)
    if col_scale is not None:
        in_specs.append(pl.BlockSpec((1, N), lambda i: (0, 0)))
        args.append(col_scale.reshape(1, N).astype(F32))
    if residual is not None:
        in_specs.append(pl.BlockSpec((tm, N), lambda i: (i, 0)))
        args.append(residual)
    kern = functools.partial(
        _norm_matmul_kernel, n_parts=len(parts), widths=widths, normed=normed,
        has_scale=col_scale is not None, has_res=residual is not None, tn=tn)
    return pl.pallas_call(
        kern,
        grid=(T // tm,),
        in_specs=in_specs,
        out_specs=pl.BlockSpec((tm, N), lambda i: (i, 0)),
        out_shape=jax.ShapeDtypeStruct((T, N), out_dtype),
        compiler_params=pltpu.CompilerParams(
            dimension_semantics=("parallel",), vmem_limit_bytes=VMEM_LIMIT),
        name=name,
    )(*args)


def _rope(xp, cos_t, sin_lo, sin_hi):
    half = MLA_ROPE // 2
    return xp * cos_t + pltpu.roll(xp, LANES - half, 1) * sin_lo + pltpu.roll(xp, half, 1) * sin_hi


def _mla_proj_kernel(x_ref, g_ref, wlat_ref, qg_ref, wuq_ref, kvg_ref, wuk_ref, wuv_ref,
                     pos_ref, freq_ref, q_ref, k_ref, v_ref, *, d_model, q_lora, kv_lora, heads, q_scale):
    xf = x_ref[...]
    h = (xf * _rms_scale(xf, d_model) * g_ref[...]).astype(BF16)
    lat = _dot(h, wlat_ref[...])
    cq = lat[:, :q_lora]
    ckv = lat[:, q_lora:q_lora + kv_lora]
    kpe = lat[:, q_lora + kv_lora:]

    half = MLA_ROPE // 2
    ang = pos_ref[...].astype(F32) * freq_ref[...]
    cos_t = jnp.cos(ang)
    sin_t = jnp.sin(ang)
    lane = lax.broadcasted_iota(jnp.int32, ang.shape, 1)
    sin_lo = jnp.where(lane < half, -sin_t, 0.0)
    sin_hi = jnp.where((lane >= half) & (lane < 2 * half), sin_t, 0.0)

    cqn = (cq * _rms_scale(cq, q_lora) * qg_ref[...]).astype(BF16)
    q = _dot(cqn, wuq_ref[...]) * q_scale
    ckvn = (ckv * _rms_scale(ckv, kv_lora) * kvg_ref[...]).astype(BF16)
    kn = _dot(ckvn, wuk_ref[...])
    v_ref[...] = _dot(ckvn, wuv_ref[...]).astype(v_ref.dtype)
    kpe_r = _rope(kpe, cos_t, sin_lo, sin_hi).astype(k_ref.dtype)
    for hd in range(heads):
        base = hd * MLA_QK_PAD
        q_ref[:, base:base + MLA_NOPE] = q[:, base:base + MLA_NOPE].astype(q_ref.dtype)
        q_ref[:, base + MLA_NOPE:base + MLA_QK_PAD] = _rope(
            q[:, base + MLA_NOPE:base + MLA_QK_PAD], cos_t, sin_lo, sin_hi).astype(q_ref.dtype)
        k_ref[:, base:base + MLA_NOPE] = kn[:, hd * MLA_NOPE:(hd + 1) * MLA_NOPE].astype(k_ref.dtype)
        k_ref[:, base + MLA_NOPE:base + MLA_QK_PAD] = kpe_r


def _mla_proj(x2d, g, wlat, qg, wuq, kvg, wuk, wuv, pos2d, freq, *, heads, tm=512):
    T, D = x2d.shape
    q_lora = qg.shape[0]
    kv_lora = kvg.shape[0]
    nlat = wlat.shape[1]
    full = lambda shape: pl.BlockSpec(shape, lambda i: (0, 0))
    kern = functools.partial(
        _mla_proj_kernel, d_model=D, q_lora=q_lora, kv_lora=kv_lora, heads=heads,
        q_scale=LOG2_E * (MLA_NOPE + MLA_ROPE) ** -0.5)
    return pl.pallas_call(
        kern,
        grid=(T // tm,),
        in_specs=[
            pl.BlockSpec((tm, D), lambda i: (i, 0)),
            full((1, D)),
            full((D, nlat)),
            full((1, q_lora)),
            full((q_lora, heads * MLA_QK_PAD)),
            full((1, kv_lora)),
            full((kv_lora, heads * MLA_NOPE)),
            full((kv_lora, heads * MLA_V)),
            pl.BlockSpec((tm, 1), lambda i: (i, 0)),
            full((1, LANES)),
        ],
        out_specs=[
            pl.BlockSpec((tm, heads * MLA_QK_PAD), lambda i: (i, 0)),
            pl.BlockSpec((tm, heads * MLA_QK_PAD), lambda i: (i, 0)),
            pl.BlockSpec((tm, heads * MLA_V), lambda i: (i, 0)),
        ],
        out_shape=[
            jax.ShapeDtypeStruct((T, heads * MLA_QK_PAD), BF16),
            jax.ShapeDtypeStruct((T, heads * MLA_QK_PAD), BF16),
            jax.ShapeDtypeStruct((T, heads * MLA_V), BF16),
        ],
        compiler_params=pltpu.CompilerParams(
            dimension_semantics=("parallel",), vmem_limit_bytes=VMEM_LIMIT),
        name="mla_proj",
    )(x2d, g.reshape(1, D), wlat, qg.reshape(1, -1), wuq, kvg.reshape(1, -1), wuk, wuv, pos2d, freq)


def _walk_key_blocks(qi, step):
    step(qi, 0, True, True)

    def quad(i, carry):
        kb = qi - 1 - 4 * i
        step(kb, 1, False, True)
        step(kb - 1, 0, False, True)
        step(kb - 2, 1, False, True)
        step(kb - 3, 0, False, True)
        return carry

    quads = lax.shift_right_logical(qi, 2)
    lax.fori_loop(0, quads, quad, 0)
    rem = qi - 4 * quads

    @pl.when(rem >= 2)
    def _():
        step(rem - 1, 1, False, True)
        step(rem - 2, 0, False, True)

    @pl.when(rem % 2 == 1)
    def _():
        step(0, 1, False, False)


def _sb_attn_kernel(q_ref, k_ref, v_ref, o_ref, acc_ref, run_ref, z_ref, *, blk, hp):
    qi = pl.program_id(2)
    row = lax.broadcasted_iota(jnp.int32, (blk, blk), 0)
    col = lax.broadcasted_iota(jnp.int32, (blk, blk), 1)
    below = row > col
    suffix = below.astype(BF16)
    heads = [slice(h * SB_HEAD_DIM, (h + 1) * SB_HEAD_DIM) for h in range(hp)]

    def logits(kb, slot):
        start = pl.multiple_of(kb * blk, blk)
        for h, c in enumerate(heads):
            z_ref[slot, h] = _dot_nt(q_ref[0, :, c], k_ref[0, pl.ds(start, blk), c])

    def step(kb, slot, masked, prefetch=True):
        if prefetch:
            logits(jnp.maximum(kb - 1, 0), 1 - slot)
        start = pl.multiple_of(kb * blk, blk)

        def suffix_sums(h):
            z = z_ref[slot, h]
            sp = jnp.maximum(z, 0.0) + jnp.log(1.0 + jnp.exp(-jnp.abs(z)))
            if masked:
                sp = jnp.where(below, sp, 0.0)
            cs = _dot(sp.astype(BF16), suffix)
            return z - sp, cs, cs[:, 0:1] + sp[:, 0:1]

        def weights_times_v(h, lb, cs, tot):
            v = v_ref[0, pl.ds(start, blk), heads[h]]
            if masked:
                w = jnp.where(below, jnp.exp(lb - cs), 0.0)
                acc_ref[h] = _dot(w.astype(BF16), v)
                run_ref[h] = tot
            else:
                run = run_ref[h]
                w = jnp.exp(lb - cs - run)
                acc_ref[h] += _dot(w.astype(BF16), v)
                run_ref[h] = run + tot

        pending = suffix_sums(0)
        for h in range(1, hp):
            nxt = suffix_sums(h)
            weights_times_v(h - 1, *pending)
            pending = nxt
        weights_times_v(hp - 1, *pending)

    logits(qi, 0)
    _walk_key_blocks(qi, step)
    for h in range(hp):
        o_ref[0, :, h * SB_HEAD_DIM:(h + 1) * SB_HEAD_DIM] = acc_ref[h].astype(o_ref.dtype)


def _sb_attn(qkv, *, heads, blk=256, hp=4):
    B, S, _ = qkv.shape
    groups = heads // hp
    wid = hp * SB_HEAD_DIM
    return pl.pallas_call(
        functools.partial(_sb_attn_kernel, blk=blk, hp=hp),
        grid=(B, groups, S // blk),
        in_specs=[
            pl.BlockSpec((1, blk, wid), lambda b, g, i: (b, i, g)),
            pl.BlockSpec((1, S, wid), lambda b, g, i: (b, 0, groups + g)),
            pl.BlockSpec((1, S, wid), lambda b, g, i: (b, 0, 2 * groups + g)),
        ],
        out_specs=pl.BlockSpec((1, blk, wid), lambda b, g, i: (b, i, g)),
        out_shape=jax.ShapeDtypeStruct((B, S, heads * SB_HEAD_DIM), BF16),
        scratch_shapes=[pltpu.VMEM((hp, blk, SB_HEAD_DIM), F32), pltpu.VMEM((hp, blk, 1), F32),
                        pltpu.VMEM((2, hp, blk, blk), F32)],
        compiler_params=pltpu.CompilerParams(
            dimension_semantics=("parallel", "parallel", "arbitrary"), vmem_limit_bytes=VMEM_LIMIT),
        name="sb_attn",
    )(qkv, qkv, qkv)


def _mla_attn_kernel(q_ref, k_ref, v_ref, o_ref, acc_ref, m_ref, z_ref, *, blk, hp):
    assert MLA_V == LANES
    qi = pl.program_id(2)
    row = lax.broadcasted_iota(jnp.int32, (blk, blk), 0)
    col = lax.broadcasted_iota(jnp.int32, (blk, blk), 1)
    visible = (col // CHUNK) <= (row // CHUNK)
    ones = jnp.ones((blk, LANES), BF16)

    def logits(kb, slot, hs=range(hp)):
        start = pl.multiple_of(kb * blk, blk)
        for h in hs:
            cols = slice(h * MLA_QK_PAD, (h + 1) * MLA_QK_PAD)
            z_ref[slot, h] = _dot_nt(q_ref[0, :, cols], k_ref[0, pl.ds(start, blk), cols])

    def logits2(kb, pair_slot, hs=range(hp)):
        logits(jnp.maximum(kb, 0), 2 * pair_slot, hs)
        logits(jnp.maximum(kb - 1, 0), 2 * pair_slot + 1, hs)

    def v_ones(kb, h):
        start = pl.multiple_of(kb * blk, blk)
        return jnp.concatenate([v_ref[0, pl.ds(start, blk), h * MLA_V:(h + 1) * MLA_V], ones], axis=1)

    def update(h, zs, v1s, first):
        zmax = zs[0] if len(zs) == 1 else jnp.maximum(zs[0], zs[1])
        if first:
            m_new = jnp.broadcast_to(jnp.max(zmax, axis=-1, keepdims=True), (blk, LANES))
        else:
            m = m_ref[h]
            m_new = jnp.maximum(m, jnp.max(zmax, axis=-1, keepdims=True))
        mm = jnp.concatenate([m_new, m_new], axis=1)
        p = jnp.concatenate([jnp.exp2(z - mm).astype(BF16) for z in zs], axis=1)
        pv = _dot(p, v1s[0] if len(zs) == 1 else jnp.concatenate(v1s, axis=0))
        if first:
            acc_ref[h] = pv
        else:
            alpha = jnp.exp2(m - m_new)
            acc_ref[h] = jnp.concatenate([alpha, alpha], axis=1) * acc_ref[h] + pv
        m_ref[h] = m_new

    def double(kb, pair_slot, prefetch=True):
        if prefetch:
            logits2(kb - 2, 1 - pair_slot)
        for h in range(hp):
            update(h, [z_ref[2 * pair_slot, h], z_ref[2 * pair_slot + 1, h]],
                   [v_ones(kb, h), v_ones(kb - 1, h)], False)

    def single(slot):
        for h in range(hp):
            update(h, [z_ref[slot, h]], [v_ones(0, h)], False)

    logits(qi, 0)
    for h in range(hp):
        update(h, [jnp.where(visible, z_ref[0, h], -1e30)], [v_ones(qi, h)], True)
        logits2(qi - 1, 1, [h])

    def trip(i, carry):
        kb = qi - 1 - 4 * i
        double(kb, 1)
        double(kb - 2, 0)
        return carry

    pairs = lax.shift_right_logical(qi, 1)
    trips = lax.shift_right_logical(pairs, 1)
    lax.fori_loop(0, trips, trip, 0)
    odd_pair = pairs % 2 == 1
    odd_block = qi % 2 == 1

    @pl.when(odd_pair)
    def _():
        double(qi - 1 - 4 * trips, 1)

    @pl.when(odd_block & odd_pair)
    def _():
        single(0)

    @pl.when(odd_block & jnp.logical_not(odd_pair))
    def _():
        single(2)

    for h in range(hp):
        o_ref[0, :, h * MLA_V:(h + 1) * MLA_V] = (
            acc_ref[h, :, :MLA_V] / acc_ref[h, :, MLA_V:]).astype(o_ref.dtype)


def _mla_attn(q, k, v, *, heads, blk=256, hp=4):
    B, S, _ = q.shape
    return pl.pallas_call(
        functools.partial(_mla_attn_kernel, blk=blk, hp=hp),
        grid=(B, heads // hp, S // blk),
        in_specs=[
            pl.BlockSpec((1, blk, hp * MLA_QK_PAD), lambda b, g, i: (b, i, g)),
            pl.BlockSpec((1, S, hp * MLA_QK_PAD), lambda b, g, i: (b, 0, g)),
            pl.BlockSpec((1, S, hp * MLA_V), lambda b, g, i: (b, 0, g)),
        ],
        out_specs=pl.BlockSpec((1, blk, hp * MLA_V), lambda b, g, i: (b, i, g)),
        out_shape=jax.ShapeDtypeStruct((B, S, heads * MLA_V), BF16),
        scratch_shapes=[pltpu.VMEM((hp, blk, 2 * MLA_V), F32), pltpu.VMEM((hp, blk, LANES), F32),
                        pltpu.VMEM((4, hp, blk, blk), F32)],
        compiler_params=pltpu.CompilerParams(
            dimension_semantics=("parallel", "parallel", "arbitrary"), vmem_limit_bytes=VMEM_LIMIT),
        name="mla_attn",
    )(q, k, v)


def _mem_block_kernel(x_ref, g_ref, wq_ref, kv_ref, wo_ref, gn_ref, o_ref, hn_ref, *,
                      heads, d_model, q_scale, tn):
    xf = x_ref[0]
    h = (xf * _rms_scale(xf, d_model) * g_ref[...]).astype(BF16)
    hd = d_model // heads
    oms = []
    for i in range(heads):
        cols = slice(i * hd, (i + 1) * hd)
        q = (_dot(h, wq_ref[:, cols]) * q_scale).astype(BF16)
        k = kv_ref[0, :, cols]
        v = kv_ref[0, :, d_model + i * hd:d_model + (i + 1) * hd]
        z = _dot_nt(q, k)
        p = jnp.exp(z - jnp.max(z, axis=-1, keepdims=True))
        l = jnp.sum(p, axis=-1, keepdims=True)
        oms.append((_dot(p.astype(BF16), v) / l).astype(BF16))
    om = jnp.concatenate(oms, axis=1)
    ys = []
    for c in range(0, d_model, tn):
        y = xf[:, c:c + tn] + _dot(om, wo_ref[:, c:c + tn])
        o_ref[0, :, c:c + tn] = y
        ys.append(y)
    y = jnp.concatenate(ys, axis=1)
    hn_ref[0] = (y * _rms_scale(y, d_model) * gn_ref[...]).astype(hn_ref.dtype)


def _mem_block(x, g, wq, kvm, wo, g_next, *, heads, tm=512, tn=512):
    B, S, D = x.shape
    M = kvm.shape[1]
    resident = lambda shape: pl.BlockSpec(shape, lambda b, i: (0, 0), pipeline_mode=pl.Buffered(1))
    vec = pl.BlockSpec((1, D), lambda b, i: (0, 0))
    tile = pl.BlockSpec((1, tm, D), lambda b, i: (b, i, 0))
    return pl.pallas_call(
        functools.partial(_mem_block_kernel, heads=heads, d_model=D,
                          q_scale=(D // heads) ** -0.5, tn=tn),
        grid=(B, S // tm),
        in_specs=[tile, vec, resident((D, D)),
                  pl.BlockSpec((1, M, 2 * D), lambda b, i: (b, 0, 0)), resident((D, D)), vec],
        out_specs=[tile, tile],
        out_shape=[jax.ShapeDtypeStruct((B, S, D), F32), jax.ShapeDtypeStruct((B, S, D), BF16)],
        compiler_params=pltpu.CompilerParams(
            dimension_semantics=("parallel", "parallel"), vmem_limit_bytes=VMEM_LIMIT),
        name="mem_block",
    )(x, g.reshape(1, D), wq, kvm, wo, g_next.reshape(1, D))


def _ffn_kernel(x_ref, h_ref, w1_ref, w2_ref, gf_ref, o_ref, acc_ref, *, d_model):
    f = pl.program_id(1)

    @pl.when(f == 0)
    def _():
        acc_ref[...] = jnp.zeros_like(acc_ref)

    a = jnp.maximum(_dot(h_ref[...], w1_ref[...]), 0.0)
    acc_ref[...] += _dot((a * a).astype(BF16), w2_ref[...])

    @pl.when(f == pl.num_programs(1) - 1)
    def _():
        y = x_ref[...] + acc_ref[...]
        o_ref[...] = y * _rms_scale(y, d_model) * gf_ref[...]


def _ffn(x2d, h2d, w1, w2, gf, *, tm=512, tf=1024):
    T, D = x2d.shape
    F = w1.shape[1]
    return pl.pallas_call(
        functools.partial(_ffn_kernel, d_model=D),
        grid=(T // tm, F // tf),
        in_specs=[
            pl.BlockSpec((tm, D), lambda i, f: (i, 0)),
            pl.BlockSpec((tm, D), lambda i, f: (i, 0)),
            pl.BlockSpec((D, tf), lambda i, f: (0, f)),
            pl.BlockSpec((tf, D), lambda i, f: (f, 0)),
            pl.BlockSpec((1, D), lambda i, f: (0, 0)),
        ],
        out_specs=pl.BlockSpec((tm, D), lambda i, f: (i, 0)),
        out_shape=jax.ShapeDtypeStruct((T, D), F32),
        scratch_shapes=[pltpu.VMEM((tm, D), F32)],
        compiler_params=pltpu.CompilerParams(
            dimension_semantics=("parallel", "arbitrary"), vmem_limit_bytes=VMEM_LIMIT),
        name="ffn",
    )(x2d, h2d, w1, w2, gf.reshape(1, D))


def _pad_cols(w, n):
    return jnp.pad(w, ((0, 0), (0, n - w.shape[1])))


def kernel(x, mem, positions, norm_mix_g, w_in, q_norm_g, w_uq, kv_norm_g, w_ukv, gn_sb_g, gn_mla_g, w_out, norm_mem_g, mem_kv_norm_g, w_mq, w_mkv, w_mo, norm_ffn_g, w_ff1, w_ff2, final_norm_g):
    B, S, D = x.shape
    M = mem.shape[1]
    T = B * S
    depth = w_in.shape[0]
    sb_width = gn_sb_g.shape[1]
    sb_heads = sb_width // SB_HEAD_DIM
    mla_heads = gn_mla_g.shape[1] // MLA_V
    q_lora = q_norm_g.shape[1]
    kv_lora = kv_norm_g.shape[1]

    half = MLA_ROPE // 2
    inv_freq = ROPE_THETA ** (-jnp.arange(half, dtype=F32) / half)
    freq = jnp.concatenate([inv_freq, inv_freq, jnp.zeros((LANES - 2 * half,), F32)]).reshape(1, LANES)
    pos2d = positions.reshape(T, 1)

    xc = x.reshape(T, D)
    for l in range(depth):
        w_in_bf = w_in[l].astype(BF16)
        w_lat = _pad_cols(w_in_bf[:, 3 * sb_width:], q_lora + kv_lora + LANES)
        wq = w_uq[l].reshape(q_lora, mla_heads, MLA_NOPE + MLA_ROPE)
        wq = jnp.pad(wq, ((0, 0), (0, 0), (0, MLA_QK_PAD - MLA_NOPE - MLA_ROPE)))
        wq = wq.reshape(q_lora, mla_heads * MLA_QK_PAD).astype(BF16)
        wkv = w_ukv[l].reshape(kv_lora, mla_heads, MLA_NOPE + MLA_V)
        wuk = wkv[:, :, :MLA_NOPE].reshape(kv_lora, mla_heads * MLA_NOPE).astype(BF16)
        wuv = wkv[:, :, MLA_NOPE:].reshape(kv_lora, mla_heads * MLA_V).astype(BF16)
        sb_scale = jnp.concatenate([
            jnp.full((sb_width,), SB_HEAD_DIM ** -0.5, F32), jnp.ones((2 * sb_width,), F32)])

        qkv = _norm_matmul([xc], [norm_mix_g[l]], w_in_bf, name="sb_qkv_proj", n_out=3 * sb_width,
                           col_scale=sb_scale)
        q_mla, k_mla, v_mla = _mla_proj(
            xc, norm_mix_g[l], w_lat, q_norm_g[l], wq, kv_norm_g[l], wuk, wuv, pos2d, freq,
            heads=mla_heads)
        o_sb = _sb_attn(qkv.reshape(B, S, 3 * sb_width), heads=sb_heads)
        o_mla = _mla_attn(q_mla.reshape(B, S, -1), k_mla.reshape(B, S, -1), v_mla.reshape(B, S, -1),
                          heads=mla_heads)
        xc = _norm_matmul(
            [o_sb.reshape(T, -1), o_mla.reshape(T, -1)], [gn_sb_g[l], gn_mla_g[l]],
            w_out[l].astype(BF16), name="out_proj", residual=xc, out_dtype=F32)

        kvm = _norm_matmul([mem.reshape(B * M, D)], [mem_kv_norm_g[l]], w_mkv[l].astype(BF16),
                           name="mem_kv_proj")
        xc, hc = _mem_block(xc.reshape(B, S, D), norm_mem_g[l], w_mq[l].astype(BF16),
                            kvm.reshape(B, M, 2 * D), w_mo[l].astype(BF16), norm_ffn_g[l],
                            heads=MEM_HEADS)

        assert depth == 1
        xc = _ffn(xc.reshape(T, D), hc.reshape(T, D), w_ff1[l].astype(BF16), w_ff2[l].astype(BF16),
                  final_norm_g)
    return xc.reshape(B, S, D)
```

```python
import functools

import jax
import jax.numpy as jnp
from jax import lax
from jax.experimental import pallas as pl
from jax.experimental.pallas import tpu as pltpu

F32 = jnp.float32
BF16 = jnp.bfloat16

EPS = 1e-6
CHUNK = 64
SB_HEAD_DIM = 128
MLA_NOPE = 128
MLA_ROPE = 64
MLA_V = 128
MLA_QK_PAD = 256
ROPE_THETA = 10000.0
LOG2_E = 1.4426950408889634
MEM_HEADS = 4
LANES = 128
VMEM_LIMIT = 56 * 1024 * 1024


def _rms_scale(xf, width):
    return lax.rsqrt(jnp.sum(xf * xf, axis=-1, keepdims=True) * (1.0 / width) + EPS)


def _dot(a, b):
    return jnp.dot(a, b, preferred_element_type=F32)


def _dot_nt(a, b):
    return lax.dot_general(a, b, (((1,), (1,)), ((), ())), preferred_element_type=F32)


def _norm_matmul_kernel(*refs, n_parts, widths, normed, has_scale, has_res, tn):
    pos = 0
    x_refs = refs[pos:pos + n_parts]; pos += n_parts
    g_refs = ()
    if normed:
        g_refs = refs[pos:pos + n_parts]; pos += n_parts
    w_ref = refs[pos]; pos += 1
    scale_ref = None
    if has_scale:
        scale_ref = refs[pos]; pos += 1
    res_ref = None
    if has_res:
        res_ref = refs[pos]; pos += 1
    o_ref = refs[pos]

    hs = []
    for p in range(n_parts):
        xf = x_refs[p][...].astype(F32)
        if normed:
            xf = xf * _rms_scale(xf, widths[p]) * g_refs[p][...]
        hs.append(xf.astype(BF16))
    h = hs[0] if n_parts == 1 else jnp.concatenate(hs, axis=1)
    for c in range(0, o_ref.shape[1], tn):
        acc = _dot(h, w_ref[:, c:c + tn])
        if has_scale:
            acc = acc * scale_ref[:, c:c + tn]
        if has_res:
            acc = acc + res_ref[:, c:c + tn]
        o_ref[:, c:c + tn] = acc.astype(o_ref.dtype)


def _norm_matmul(parts, gains, w, *, name, n_out=None, col_scale=None, residual=None, out_dtype=BF16,
                 tm=512, tn=512):
    T = parts[0].shape[0]
    widths = tuple(p.shape[1] for p in parts)
    K = sum(widths)
    N = w.shape[1] if n_out is None else n_out
    tm = min(tm, T)
    tn = min(tn, N)
    assert T % tm == 0 and N % tn == 0 and w.shape[0] == K
    normed = gains is not None
    in_specs = [pl.BlockSpec((tm, k), lambda i: (i, 0)) for k in widths]
    args = list(parts)
    if normed:
        in_specs += [pl.BlockSpec((1, k), lambda i: (0, 0)) for k in widths]
        args += [g.reshape(1, -1).astype(F32) for g in gains]
    in_specs.append(pl.BlockSpec((K, N), lambda i: (0, 0), pipeline_mode=pl.Buffered(1)))
    args.append(w)
    if col_scale is not None:
        in_specs.append(pl.BlockSpec((1, N), lambda i: (0, 0)))
        args.append(col_scale.reshape(1, N).astype(F32))
    if residual is not None:
        in_specs.append(pl.BlockSpec((tm, N), lambda i: (i, 0)))
        args.append(residual)
    kern = functools.partial(
        _norm_matmul_kernel, n_parts=len(parts), widths=widths, normed=normed,
        has_scale=col_scale is not None, has_res=residual is not None, tn=tn)
    return pl.pallas_call(
        kern,
        grid=(T // tm,),
        in_specs=in_specs,
        out_specs=pl.BlockSpec((tm, N), lambda i: (i, 0)),
        out_shape=jax.ShapeDtypeStruct((T, N), out_dtype),
        compiler_params=pltpu.CompilerParams(
            dimension_semantics=("parallel",), vmem_limit_bytes=VMEM_LIMIT),
        name=name,
    )(*args)


def _rope(xp, cos_t, sin_lo, sin_hi):
    half = MLA_ROPE // 2
    return xp * cos_t + pltpu.roll(xp, LANES - half, 1) * sin_lo + pltpu.roll(xp, half, 1) * sin_hi


def _in_proj_kernel(x_ref, g_ref, wsb_ref, sbs_ref, wlat_ref, qg_ref, wuq_ref, kvg_ref, wuk_ref, wuv_ref,
                    pos_ref, freq_ref, qkv_ref, q_ref, k_ref, v_ref, *,
                    d_model, q_lora, kv_lora, heads, q_scale, tn):
    xf = x_ref[...]
    h = (xf * _rms_scale(xf, d_model) * g_ref[...]).astype(BF16)
    lat = _dot(h, wlat_ref[...])
    cq = lat[:, :q_lora]
    ckv = lat[:, q_lora:q_lora + kv_lora]
    kpe = lat[:, q_lora + kv_lora:]

    def sb_chunks(lo, hi):
        for c in range(lo, hi, tn):
            qkv_ref[:, c:c + tn] = (_dot(h, wsb_ref[:, c:c + tn]) * sbs_ref[:, c:c + tn]).astype(qkv_ref.dtype)

    n_sb = qkv_ref.shape[1]
    sb_chunks(0, n_sb // 2)

    half = MLA_ROPE // 2
    ang = pos_ref[...].astype(F32) * freq_ref[...]
    cos_t = jnp.cos(ang)
    sin_t = jnp.sin(ang)
    lane = lax.broadcasted_iota(jnp.int32, ang.shape, 1)
    sin_lo = jnp.where(lane < half, -sin_t, 0.0)
    sin_hi = jnp.where((lane >= half) & (lane < 2 * half), sin_t, 0.0)

    cqn = (cq * _rms_scale(cq, q_lora) * qg_ref[...]).astype(BF16)
    q = _dot(cqn, wuq_ref[...]) * q_scale
    ckvn = (ckv * _rms_scale(ckv, kv_lora) * kvg_ref[...]).astype(BF16)
    kn = _dot(ckvn, wuk_ref[...])
    v_ref[...] = _dot(ckvn, wuv_ref[...]).astype(v_ref.dtype)
    sb_chunks(n_sb // 2, n_sb)
    kpe_r = _rope(kpe, cos_t, sin_lo, sin_hi).astype(k_ref.dtype)
    for hd in range(heads):
        base = hd * MLA_QK_PAD
        q_ref[:, base:base + MLA_NOPE] = q[:, base:base + MLA_NOPE].astype(q_ref.dtype)
        q_ref[:, base + MLA_NOPE:base + MLA_QK_PAD] = _rope(
            q[:, base + MLA_NOPE:base + MLA_QK_PAD], cos_t, sin_lo, sin_hi).astype(q_ref.dtype)
        k_ref[:, base:base + MLA_NOPE] = kn[:, hd * MLA_NOPE:(hd + 1) * MLA_NOPE].astype(k_ref.dtype)
        k_ref[:, base + MLA_NOPE:base + MLA_QK_PAD] = kpe_r


def _in_proj(x2d, g, w_in_bf, n_sb, sb_scale, wlat, qg, wuq, kvg, wuk, wuv, pos2d, freq, *,
             heads, tm=512, tn=512):
    T, D = x2d.shape
    q_lora = qg.shape[0]
    kv_lora = kvg.shape[0]
    nlat = wlat.shape[1]
    vec = lambda n: pl.BlockSpec((1, n), lambda i: (0, 0))
    resident = lambda shape: pl.BlockSpec(shape, lambda i: (0, 0), pipeline_mode=pl.Buffered(1))
    rows = lambda n: pl.BlockSpec((tm, n), lambda i: (i, 0))
    kern = functools.partial(
        _in_proj_kernel, d_model=D, q_lora=q_lora, kv_lora=kv_lora, heads=heads,
        q_scale=LOG2_E * (MLA_NOPE + MLA_ROPE) ** -0.5, tn=tn)
    assert n_sb % (2 * tn) == 0
    return pl.pallas_call(
        kern,
        grid=(T // tm,),
        in_specs=[
            rows(D), vec(D), resident((D, n_sb)), vec(n_sb), resident((D, nlat)),
            vec(q_lora), resident((q_lora, heads * MLA_QK_PAD)),
            vec(kv_lora), resident((kv_lora, heads * MLA_NOPE)), resident((kv_lora, heads * MLA_V)),
            rows(1), vec(LANES),
        ],
        out_specs=[rows(n_sb), rows(heads * MLA_QK_PAD), rows(heads * MLA_QK_PAD), rows(heads * MLA_V)],
        out_shape=[
            jax.ShapeDtypeStruct((T, n_sb), BF16),
            jax.ShapeDtypeStruct((T, heads * MLA_QK_PAD), BF16),
            jax.ShapeDtypeStruct((T, heads * MLA_QK_PAD), BF16),
            jax.ShapeDtypeStruct((T, heads * MLA_V), BF16),
        ],
        compiler_params=pltpu.CompilerParams(
            dimension_semantics=("parallel",), vmem_limit_bytes=VMEM_LIMIT),
        name="in_proj",
    )(x2d, g.reshape(1, D), w_in_bf, sb_scale.reshape(1, n_sb), wlat, qg.reshape(1, -1), wuq,
      kvg.reshape(1, -1), wuk, wuv, pos2d, freq)


def _walk_key_blocks(qi, step):
    step(qi, 0, True, True)

    def quad(i, carry):
        kb = qi - 1 - 4 * i
        step(kb, 1, False, True)
        step(kb - 1, 0, False, True)
        step(kb - 2, 1, False, True)
        step(kb - 3, 0, False, True)
        return carry

    quads = lax.shift_right_logical(qi, 2)
    lax.fori_loop(0, quads, quad, 0)
    rem = qi - 4 * quads

    @pl.when(rem >= 2)
    def _():
        step(rem - 1, 1, False, True)
        step(rem - 2, 0, False, True)

    @pl.when(rem % 2 == 1)
    def _():
        step(0, 1, False, False)


def _sb_attn_kernel(q_ref, k_ref, v_ref, o_ref, acc_ref, run_ref, z_ref, *, blk, hp):
    qi = pl.program_id(2)
    row = lax.broadcasted_iota(jnp.int32, (blk, blk), 0)
    col = lax.broadcasted_iota(jnp.int32, (blk, blk), 1)
    below = row > col
    suffix = below.astype(BF16)
    heads = [slice(h * SB_HEAD_DIM, (h + 1) * SB_HEAD_DIM) for h in range(hp)]

    def logits(kb, slot):
        start = pl.multiple_of(kb * blk, blk)
        for h, c in enumerate(heads):
            z_ref[slot, h] = _dot_nt(q_ref[0, :, c], k_ref[0, pl.ds(start, blk), c])

    def step(kb, slot, masked, prefetch=True):
        if prefetch:
            logits(jnp.maximum(kb - 1, 0), 1 - slot)
        start = pl.multiple_of(kb * blk, blk)

        def suffix_sums(h):
            z = z_ref[slot, h]
            sp = jnp.maximum(z, 0.0) + jnp.log(1.0 + jnp.exp(-jnp.abs(z)))
            if masked:
                sp = jnp.where(below, sp, 0.0)
            cs = _dot(sp.astype(BF16), suffix)
            return z - sp, cs, cs[:, 0:1] + sp[:, 0:1]

        def weights_times_v(h, lb, cs, tot):
            v = v_ref[0, pl.ds(start, blk), heads[h]]
            if masked:
                w = jnp.where(below, jnp.exp(lb - cs), 0.0)
                acc_ref[h] = _dot(w.astype(BF16), v)
                run_ref[h] = tot
            else:
                run = run_ref[h]
                w = jnp.exp(lb - cs - run)
                acc_ref[h] += _dot(w.astype(BF16), v)
                run_ref[h] = run + tot

        sums = [suffix_sums(h) for h in range(hp)]
        for h in range(hp):
            weights_times_v(h, *sums[h])

    logits(qi, 0)
    _walk_key_blocks(qi, step)
    for h in range(hp):
        o_ref[0, :, h * SB_HEAD_DIM:(h + 1) * SB_HEAD_DIM] = acc_ref[h].astype(o_ref.dtype)


def _sb_attn(qkv, *, heads, blk=256, hp=4):
    B, S, _ = qkv.shape
    groups = heads // hp
    wid = hp * SB_HEAD_DIM
    return pl.pallas_call(
        functools.partial(_sb_attn_kernel, blk=blk, hp=hp),
        grid=(B, groups, S // blk),
        in_specs=[
            pl.BlockSpec((1, blk, wid), lambda b, g, i: (b, i, g)),
            pl.BlockSpec((1, S, wid), lambda b, g, i: (b, 0, groups + g)),
            pl.BlockSpec((1, S, wid), lambda b, g, i: (b, 0, 2 * groups + g)),
        ],
        out_specs=pl.BlockSpec((1, blk, wid), lambda b, g, i: (b, i, g)),
        out_shape=jax.ShapeDtypeStruct((B, S, heads * SB_HEAD_DIM), BF16),
        scratch_shapes=[pltpu.VMEM((hp, blk, SB_HEAD_DIM), F32), pltpu.VMEM((hp, blk, 1), F32),
                        pltpu.VMEM((2, hp, blk, blk), F32)],
        compiler_params=pltpu.CompilerParams(
            dimension_semantics=("parallel", "parallel", "arbitrary"), vmem_limit_bytes=VMEM_LIMIT),
        name="sb_attn",
    )(qkv, qkv, qkv)


def _mla_attn_kernel(q_ref, k_ref, v_ref, o_ref, acc_ref, m_ref, z_ref, *, blk, hp):
    assert MLA_V == LANES
    qi = pl.program_id(2)
    row = lax.broadcasted_iota(jnp.int32, (blk, blk), 0)
    col = lax.broadcasted_iota(jnp.int32, (blk, blk), 1)
    visible = (col // CHUNK) <= (row // CHUNK)
    ones = jnp.ones((blk, LANES), BF16)

    def logits(kb, slot, hs=range(hp)):
        start = pl.multiple_of(kb * blk, blk)
        for h in hs:
            cols = slice(h * MLA_QK_PAD, (h + 1) * MLA_QK_PAD)
            z_ref[slot, h] = _dot_nt(q_ref[0, :, cols], k_ref[0, pl.ds(start, blk), cols])

    def logits2(kb, pair_slot, hs=range(hp)):
        logits(jnp.maximum(kb, 0), 2 * pair_slot, hs)
        logits(jnp.maximum(kb - 1, 0), 2 * pair_slot + 1, hs)

    def v_ones(kb, h):
        start = pl.multiple_of(kb * blk, blk)
        return jnp.concatenate([v_ref[0, pl.ds(start, blk), h * MLA_V:(h + 1) * MLA_V], ones], axis=1)

    def update(h, zs, v1s, first):
        zmax = zs[0] if len(zs) == 1 else jnp.maximum(zs[0], zs[1])
        if first:
            m_new = jnp.broadcast_to(jnp.max(zmax, axis=-1, keepdims=True), (blk, LANES))
        else:
            m = m_ref[h]
            m_new = jnp.maximum(m, jnp.max(zmax, axis=-1, keepdims=True))
        mm = jnp.concatenate([m_new, m_new], axis=1)
        p = jnp.concatenate([jnp.exp2(z - mm).astype(BF16) for z in zs], axis=1)
        pv = _dot(p, v1s[0] if len(zs) == 1 else jnp.concatenate(v1s, axis=0))
        if first:
            acc_ref[h] = pv
        else:
            alpha = jnp.exp2(m - m_new)
            acc_ref[h] = jnp.concatenate([alpha, alpha], axis=1) * acc_ref[h] + pv
        m_ref[h] = m_new

    def double(kb, pair_slot, prefetch=True):
        if prefetch:
            logits2(kb - 2, 1 - pair_slot)
        for h in range(hp):
            update(h, [z_ref[2 * pair_slot, h], z_ref[2 * pair_slot + 1, h]],
                   [v_ones(kb, h), v_ones(kb - 1, h)], False)

    def single(slot):
        for h in range(hp):
            update(h, [z_ref[slot, h]], [v_ones(0, h)], False)

    logits(qi, 0)
    for h in range(hp):
        update(h, [jnp.where(visible, z_ref[0, h], -1e30)], [v_ones(qi, h)], True)
        logits2(qi - 1, 1, [h])

    def trip(i, carry):
        kb = qi - 1 - 4 * i
        double(kb, 1)
        double(kb - 2, 0)
        return carry

    pairs = lax.shift_right_logical(qi, 1)
    trips = lax.shift_right_logical(pairs, 1)
    lax.fori_loop(0, trips, trip, 0)
    odd_pair = pairs % 2 == 1
    odd_block = qi % 2 == 1

    @pl.when(odd_pair)
    def _():
        double(qi - 1 - 4 * trips, 1)

    @pl.when(odd_block & odd_pair)
    def _():
        single(0)

    @pl.when(odd_block & jnp.logical_not(odd_pair))
    def _():
        single(2)

    for h in range(hp):
        o_ref[0, :, h * MLA_V:(h + 1) * MLA_V] = (
            acc_ref[h, :, :MLA_V] / acc_ref[h, :, MLA_V:]).astype(o_ref.dtype)


def _mla_attn(q, k, v, *, heads, blk=256, hp=4):
    B, S, _ = q.shape
    return pl.pallas_call(
        functools.partial(_mla_attn_kernel, blk=blk, hp=hp),
        grid=(B, heads // hp, S // blk),
        in_specs=[
            pl.BlockSpec((1, blk, hp * MLA_QK_PAD), lambda b, g, i: (b, i, g)),
            pl.BlockSpec((1, S, hp * MLA_QK_PAD), lambda b, g, i: (b, 0, g)),
            pl.BlockSpec((1, S, hp * MLA_V), lambda b, g, i: (b, 0, g)),
        ],
        out_specs=pl.BlockSpec((1, blk, hp * MLA_V), lambda b, g, i: (b, i, g)),
        out_shape=jax.ShapeDtypeStruct((B, S, heads * MLA_V), BF16),
        scratch_shapes=[pltpu.VMEM((hp, blk, 2 * MLA_V), F32), pltpu.VMEM((hp, blk, LANES), F32),
                        pltpu.VMEM((4, hp, blk, blk), F32)],
        compiler_params=pltpu.CompilerParams(
            dimension_semantics=("parallel", "parallel", "arbitrary"), vmem_limit_bytes=VMEM_LIMIT),
        name="mla_attn",
    )(q, k, v)


def _mem_block_kernel(x_ref, g_ref, wq_ref, kv_ref, wo_ref, gn_ref, o_ref, hn_ref, *,
                      heads, d_model, q_scale, tn):
    xf = x_ref[0]
    h = (xf * _rms_scale(xf, d_model) * g_ref[...]).astype(BF16)
    hd = d_model // heads
    oms = []
    for i in range(heads):
        cols = slice(i * hd, (i + 1) * hd)
        q = (_dot(h, wq_ref[:, cols]) * q_scale).astype(BF16)
        k = kv_ref[0, :, cols]
        v = kv_ref[0, :, d_model + i * hd:d_model + (i + 1) * hd]
        z = _dot_nt(q, k)
        p = jnp.exp(z - jnp.max(z, axis=-1, keepdims=True))
        l = jnp.sum(p, axis=-1, keepdims=True)
        oms.append((_dot(p.astype(BF16), v) / l).astype(BF16))
    om = jnp.concatenate(oms, axis=1)
    ys = []
    for c in range(0, d_model, tn):
        y = xf[:, c:c + tn] + _dot(om, wo_ref[:, c:c + tn])
        o_ref[0, :, c:c + tn] = y
        ys.append(y)
    y = jnp.concatenate(ys, axis=1)
    hn_ref[0] = (y * _rms_scale(y, d_model) * gn_ref[...]).astype(hn_ref.dtype)


def _mem_block(x, g, wq, kvm, wo, g_next, *, heads, tm=512, tn=512):
    B, S, D = x.shape
    M = kvm.shape[1]
    resident = lambda shape: pl.BlockSpec(shape, lambda b, i: (0, 0), pipeline_mode=pl.Buffered(1))
    vec = pl.BlockSpec((1, D), lambda b, i: (0, 0))
    tile = pl.BlockSpec((1, tm, D), lambda b, i: (b, i, 0))
    return pl.pallas_call(
        functools.partial(_mem_block_kernel, heads=heads, d_model=D,
                          q_scale=(D // heads) ** -0.5, tn=tn),
        grid=(B, S // tm),
        in_specs=[tile, vec, resident((D, D)),
                  pl.BlockSpec((1, M, 2 * D), lambda b, i: (b, 0, 0)), resident((D, D)), vec],
        out_specs=[tile, tile],
        out_shape=[jax.ShapeDtypeStruct((B, S, D), F32), jax.ShapeDtypeStruct((B, S, D), BF16)],
        compiler_params=pltpu.CompilerParams(
            dimension_semantics=("parallel", "parallel"), vmem_limit_bytes=VMEM_LIMIT),
        name="mem_block",
    )(x, g.reshape(1, D), wq, kvm, wo, g_next.reshape(1, D))


def _ffn_kernel(x_ref, h_ref, w1_ref, w2_ref, gf_ref, o_ref, acc_ref, *, d_model):
    f = pl.program_id(1)

    @pl.when(f == 0)
    def _():
        acc_ref[...] = jnp.zeros_like(acc_ref)

    a = jnp.maximum(_dot(h_ref[...], w1_ref[...]), 0.0)
    acc_ref[...] += _dot((a * a).astype(BF16), w2_ref[...])

    @pl.when(f == pl.num_programs(1) - 1)
    def _():
        y = x_ref[...] + acc_ref[...]
        o_ref[...] = y * _rms_scale(y, d_model) * gf_ref[...]


def _ffn(x2d, h2d, w1, w2, gf, *, tm=512, tf=1024):
    T, D = x2d.shape
    F = w1.shape[1]
    return pl.pallas_call(
        functools.partial(_ffn_kernel, d_model=D),
        grid=(T // tm, F // tf),
        in_specs=[
            pl.BlockSpec((tm, D), lambda i, f: (i, 0)),
            pl.BlockSpec((tm, D), lambda i, f: (i, 0)),
            pl.BlockSpec((D, tf), lambda i, f: (0, f)),
            pl.BlockSpec((tf, D), lambda i, f: (f, 0)),
            pl.BlockSpec((1, D), lambda i, f: (0, 0)),
        ],
        out_specs=pl.BlockSpec((tm, D), lambda i, f: (i, 0)),
        out_shape=jax.ShapeDtypeStruct((T, D), F32),
        scratch_shapes=[pltpu.VMEM((tm, D), F32)],
        compiler_params=pltpu.CompilerParams(
            dimension_semantics=("parallel", "arbitrary"), vmem_limit_bytes=VMEM_LIMIT),
        name="ffn",
    )(x2d, h2d, w1, w2, gf.reshape(1, D))


def _pad_cols(w, n):
    return jnp.pad(w, ((0, 0), (0, n - w.shape[1])))


def kernel(x, mem, positions, norm_mix_g, w_in, q_norm_g, w_uq, kv_norm_g, w_ukv, gn_sb_g, gn_mla_g, w_out, norm_mem_g, mem_kv_norm_g, w_mq, w_mkv, w_mo, norm_ffn_g, w_ff1, w_ff2, final_norm_g):
    B, S, D = x.shape
    M = mem.shape[1]
    T = B * S
    depth = w_in.shape[0]
    sb_width = gn_sb_g.shape[1]
    sb_heads = sb_width // SB_HEAD_DIM
    mla_heads = gn_mla_g.shape[1] // MLA_V
    q_lora = q_norm_g.shape[1]
    kv_lora = kv_norm_g.shape[1]

    half = MLA_ROPE // 2
    inv_freq = ROPE_THETA ** (-jnp.arange(half, dtype=F32) / half)
    freq = jnp.concatenate([inv_freq, inv_freq, jnp.zeros((LANES - 2 * half,), F32)]).reshape(1, LANES)
    pos2d = positions.reshape(T, 1)

    xc = x.reshape(T, D)
    for l in range(depth):
        w_in_bf = w_in[l].astype(BF16)
        w_lat = _pad_cols(w_in_bf[:, 3 * sb_width:], q_lora + kv_lora + LANES)
        wq = w_uq[l].reshape(q_lora, mla_heads, MLA_NOPE + MLA_ROPE)
        wq = jnp.pad(wq, ((0, 0), (0, 0), (0, MLA_QK_PAD - MLA_NOPE - MLA_ROPE)))
        wq = wq.reshape(q_lora, mla_heads * MLA_QK_PAD).astype(BF16)
        wkv = w_ukv[l].reshape(kv_lora, mla_heads, MLA_NOPE + MLA_V)
        wuk = wkv[:, :, :MLA_NOPE].reshape(kv_lora, mla_heads * MLA_NOPE).astype(BF16)
        wuv = wkv[:, :, MLA_NOPE:].reshape(kv_lora, mla_heads * MLA_V).astype(BF16)
        sb_scale = jnp.concatenate([
            jnp.full((sb_width,), SB_HEAD_DIM ** -0.5, F32), jnp.ones((2 * sb_width,), F32)])

        qkv, q_mla, k_mla, v_mla = _in_proj(
            xc, norm_mix_g[l], w_in_bf, 3 * sb_width, sb_scale, w_lat, q_norm_g[l], wq, kv_norm_g[l],
            wuk, wuv, pos2d, freq, heads=mla_heads)
        o_sb = _sb_attn(qkv.reshape(B, S, 3 * sb_width), heads=sb_heads)
        o_mla = _mla_attn(q_mla.reshape(B, S, -1), k_mla.reshape(B, S, -1), v_mla.reshape(B, S, -1),
                          heads=mla_heads)
        xc = _norm_matmul(
            [o_sb.reshape(T, -1), o_mla.reshape(T, -1)], [gn_sb_g[l], gn_mla_g[l]],
            w_out[l].astype(BF16), name="out_proj", residual=xc, out_dtype=F32)

        kvm = _norm_matmul([mem.reshape(B * M, D)], [mem_kv_norm_g[l]], w_mkv[l].astype(BF16),
                           name="mem_kv_proj")
        xc, hc = _mem_block(xc.reshape(B, S, D), norm_mem_g[l], w_mq[l].astype(BF16),
                            kvm.reshape(B, M, 2 * D), w_mo[l].astype(BF16), norm_ffn_g[l],
                            heads=MEM_HEADS)

        assert depth == 1
        xc = _ffn(xc.reshape(T, D), hc.reshape(T, D), w_ff1[l].astype(BF16), w_ff2[l].astype(BF16),
                  final_norm_g)
    return xc.reshape(B, S, D)
```

```python
import functools

import jax
import jax.numpy as jnp
from jax import lax
from jax.experimental import pallas as pl
from jax.experimental.pallas import tpu as pltpu

F32 = jnp.float32
BF16 = jnp.bfloat16

EPS = 1e-6
CHUNK = 64
SB_HEAD_DIM = 128
MLA_NOPE = 128
MLA_ROPE = 64
MLA_V = 128
MLA_QK_PAD = 256
ROPE_THETA = 10000.0
LOG2_E = 1.4426950408889634
MEM_HEADS = 4
LANES = 128
VMEM_LIMIT = 56 * 1024 * 1024


def _rms_scale(xf, width):
    return lax.rsqrt(jnp.sum(xf * xf, axis=-1, keepdims=True) * (1.0 / width) + EPS)


def _dot(a, b):
    return jnp.dot(a, b, preferred_element_type=F32)


def _dot_nt(a, b):
    return lax.dot_general(a, b, (((1,), (1,)), ((), ())), preferred_element_type=F32)


def _norm_matmul_kernel(*refs, n_parts, widths, normed, has_scale, has_res, tn):
    pos = 0
    x_refs = refs[pos:pos + n_parts]; pos += n_parts
    g_refs = ()
    if normed:
        g_refs = refs[pos:pos + n_parts]; pos += n_parts
    w_ref = refs[pos]; pos += 1
    scale_ref = None
    if has_scale:
        scale_ref = refs[pos]; pos += 1
    res_ref = None
    if has_res:
        res_ref = refs[pos]; pos += 1
    o_ref = refs[pos]

    hs = []
    for p in range(n_parts):
        xf = x_refs[p][...].astype(F32)
        if normed:
            xf = xf * _rms_scale(xf, widths[p]) * g_refs[p][...]
        hs.append(xf.astype(BF16))
    h = hs[0] if n_parts == 1 else jnp.concatenate(hs, axis=1)
    for c in range(0, o_ref.shape[1], tn):
        acc = _dot(h, w_ref[:, c:c + tn])
        if has_scale:
            acc = acc * scale_ref[:, c:c + tn]
        if has_res:
            acc = acc + res_ref[:, c:c + tn]
        o_ref[:, c:c + tn] = acc.astype(o_ref.dtype)


def _norm_matmul(parts, gains, w, *, name, n_out=None, col_scale=None, residual=None, out_dtype=BF16,
                 tm=512, tn=512):
    T = parts[0].shape[0]
    widths = tuple(p.shape[1] for p in parts)
    K = sum(widths)
    N = w.shape[1] if n_out is None else n_out
    tm = min(tm, T)
    tn = min(tn, N)
    assert T % tm == 0 and N % tn == 0 and w.shape[0] == K
    normed = gains is not None
    in_specs = [pl.BlockSpec((tm, k), lambda i: (i, 0)) for k in widths]
    args = list(parts)
    if normed:
        in_specs += [pl.BlockSpec((1, k), lambda i: (0, 0)) for k in widths]
        args += [g.reshape(1, -1).astype(F32) for g in gains]
    in_specs.append(pl.BlockSpec((K, N), lambda i: (0, 0), pipeline_mode=pl.Buffered(1)))
    args.append(w)
    if col_scale is not None:
        in_specs.append(pl.BlockSpec((1, N), lambda i: (0, 0)))
        args.append(col_scale.reshape(1, N).astype(F32))
    if residual is not None:
        in_specs.append(pl.BlockSpec((tm, N), lambda i: (i, 0)))
        args.append(residual)
    kern = functools.partial(
        _norm_matmul_kernel, n_parts=len(parts), widths=widths, normed=normed,
        has_scale=col_scale is not None, has_res=residual is not None, tn=tn)
    return pl.pallas_call(
        kern,
        grid=(T // tm,),
        in_specs=in_specs,
        out_specs=pl.BlockSpec((tm, N), lambda i: (i, 0)),
        out_shape=jax.ShapeDtypeStruct((T, N), out_dtype),
        compiler_params=pltpu.CompilerParams(
            dimension_semantics=("parallel",), vmem_limit_bytes=VMEM_LIMIT),
        name=name,
    )(*args)


def _rope(xp, cos_t, sin_lo, sin_hi):
    half = MLA_ROPE // 2
    return xp * cos_t + pltpu.roll(xp, LANES - half, 1) * sin_lo + pltpu.roll(xp, half, 1) * sin_hi


def _in_proj_kernel(x_ref, g_ref, wsb_ref, sbs_ref, wlat_ref, qg_ref, wuq_ref, kvg_ref, wuk_ref, wuv_ref,
                    pos_ref, freq_ref, qkv_ref, q_ref, k_ref, v_ref, *,
                    d_model, q_lora, kv_lora, heads, q_scale, tn):
    xf = x_ref[...]
    h = (xf * _rms_scale(xf, d_model) * g_ref[...]).astype(BF16)
    lat = _dot(h, wlat_ref[...])
    cq = lat[:, :q_lora]
    ckv = lat[:, q_lora:q_lora + kv_lora]
    kpe = lat[:, q_lora + kv_lora:]

    def sb_chunks(lo, hi):
        for c in range(lo, hi, tn):
            qkv_ref[:, c:c + tn] = (_dot(h, wsb_ref[:, c:c + tn]) * sbs_ref[:, c:c + tn]).astype(qkv_ref.dtype)

    n_sb = qkv_ref.shape[1]
    sb_chunks(0, n_sb // 2)

    half = MLA_ROPE // 2
    ang = pos_ref[...].astype(F32) * freq_ref[...]
    cos_t = jnp.cos(ang)
    sin_t = jnp.sin(ang)
    lane = lax.broadcasted_iota(jnp.int32, ang.shape, 1)
    sin_lo = jnp.where(lane < half, -sin_t, 0.0)
    sin_hi = jnp.where((lane >= half) & (lane < 2 * half), sin_t, 0.0)

    cqn = (cq * _rms_scale(cq, q_lora) * qg_ref[...]).astype(BF16)
    q = _dot(cqn, wuq_ref[...]) * q_scale
    ckvn = (ckv * _rms_scale(ckv, kv_lora) * kvg_ref[...]).astype(BF16)
    kn = _dot(ckvn, wuk_ref[...])
    v_ref[...] = _dot(ckvn, wuv_ref[...]).astype(v_ref.dtype)
    sb_chunks(n_sb // 2, n_sb)
    kpe_r = _rope(kpe, cos_t, sin_lo, sin_hi).astype(k_ref.dtype)
    for hd in range(heads):
        base = hd * MLA_QK_PAD
        q_ref[:, base:base + MLA_NOPE] = q[:, base:base + MLA_NOPE].astype(q_ref.dtype)
        q_ref[:, base + MLA_NOPE:base + MLA_QK_PAD] = _rope(
            q[:, base + MLA_NOPE:base + MLA_QK_PAD], cos_t, sin_lo, sin_hi).astype(q_ref.dtype)
        k_ref[:, base:base + MLA_NOPE] = kn[:, hd * MLA_NOPE:(hd + 1) * MLA_NOPE].astype(k_ref.dtype)
        k_ref[:, base + MLA_NOPE:base + MLA_QK_PAD] = kpe_r


def _in_proj(x2d, g, w_in_bf, n_sb, sb_scale, wlat, qg, wuq, kvg, wuk, wuv, pos2d, freq, *,
             heads, tm=512, tn=512):
    T, D = x2d.shape
    q_lora = qg.shape[0]
    kv_lora = kvg.shape[0]
    nlat = wlat.shape[1]
    vec = lambda n: pl.BlockSpec((1, n), lambda i: (0, 0))
    resident = lambda shape: pl.BlockSpec(shape, lambda i: (0, 0), pipeline_mode=pl.Buffered(1))
    rows = lambda n: pl.BlockSpec((tm, n), lambda i: (i, 0))
    kern = functools.partial(
        _in_proj_kernel, d_model=D, q_lora=q_lora, kv_lora=kv_lora, heads=heads,
        q_scale=LOG2_E * (MLA_NOPE + MLA_ROPE) ** -0.5, tn=tn)
    assert n_sb % (2 * tn) == 0
    return pl.pallas_call(
        kern,
        grid=(T // tm,),
        in_specs=[
            rows(D), vec(D), resident((D, n_sb)), vec(n_sb), resident((D, nlat)),
            vec(q_lora), resident((q_lora, heads * MLA_QK_PAD)),
            vec(kv_lora), resident((kv_lora, heads * MLA_NOPE)), resident((kv_lora, heads * MLA_V)),
            rows(1), vec(LANES),
        ],
        out_specs=[rows(n_sb), rows(heads * MLA_QK_PAD), rows(heads * MLA_QK_PAD), rows(heads * MLA_V)],
        out_shape=[
            jax.ShapeDtypeStruct((T, n_sb), BF16),
            jax.ShapeDtypeStruct((T, heads * MLA_QK_PAD), BF16),
            jax.ShapeDtypeStruct((T, heads * MLA_QK_PAD), BF16),
            jax.ShapeDtypeStruct((T, heads * MLA_V), BF16),
        ],
        compiler_params=pltpu.CompilerParams(
            dimension_semantics=("parallel",), vmem_limit_bytes=VMEM_LIMIT),
        name="in_proj",
    )(x2d, g.reshape(1, D), w_in_bf, sb_scale.reshape(1, n_sb), wlat, qg.reshape(1, -1), wuq,
      kvg.reshape(1, -1), wuk, wuv, pos2d, freq)


def _interleave(a, b):
    merged = sorted([((i + 0.5) / len(a), 0, i) for i in range(len(a))] +
                    [((i + 0.5) / len(b), 1, i) for i in range(len(b))])
    for _, which, i in merged:
        (a, b)[which][i]()


def _sb_program(q_ref, k_ref, v_ref, acc_ref, run_ref, z_ref, *, blk, hp):
    row_chunk = blk
    row = lax.broadcasted_iota(jnp.int32, (blk, blk), 0)
    col = lax.broadcasted_iota(jnp.int32, (blk, blk), 1)
    below = row > col
    suffix = below.astype(BF16)
    heads = [slice(h * SB_HEAD_DIM, (h + 1) * SB_HEAD_DIM) for h in range(hp)]

    def logits(kb, slot):
        start = pl.multiple_of(kb * blk, blk)

        def one(h):
            def run():
                z_ref[slot, h] = _dot_nt(q_ref[0, :, heads[h]], k_ref[0, pl.ds(start, blk), heads[h]])
            return run
        return [one(h) for h in range(hp)]

    def step(kb, slot, masked, prefetch=True):
        start = pl.multiple_of(kb * blk, blk)
        parts = [(h, slice(r, r + row_chunk)) for h in range(hp) for r in range(0, blk, row_chunk)]
        sums = {}

        def suffix_sums(h, rows):
            def run():
                z = z_ref[slot, h, rows, :]
                sp = jnp.maximum(z, 0.0) + jnp.log(1.0 + jnp.exp(-jnp.abs(z)))
                if masked:
                    sp = jnp.where(below[rows], sp, 0.0)
                cs = _dot(sp.astype(BF16), suffix)
                sums[h, rows.start] = (z - sp, cs, cs[:, 0:1] + sp[:, 0:1])
            return run

        def weights_times_v(h, rows):
            def run():
                lb, cs, tot = sums[h, rows.start]
                v = v_ref[0, pl.ds(start, blk), heads[h]]
                if masked:
                    w = jnp.where(below[rows], jnp.exp(lb - cs), 0.0)
                    acc_ref[h, rows, :] = _dot(w.astype(BF16), v)
                    run_ref[h, rows, :] = tot
                else:
                    total = run_ref[h, rows, :]
                    w = jnp.exp(lb - cs - total)
                    acc_ref[h, rows, :] += _dot(w.astype(BF16), v)
                    run_ref[h, rows, :] = total + tot
            return run

        items = logits(jnp.maximum(kb - 1, 0), 1 - slot) if prefetch else []
        return items + [suffix_sums(*p) for p in parts] + [weights_times_v(*p) for p in parts]

    return logits, step


def _mla_program(q_ref, k_ref, v_ref, acc_ref, m_ref, z_ref, *, blk, hp):
    assert MLA_V == LANES
    row = lax.broadcasted_iota(jnp.int32, (blk, blk), 0)
    col = lax.broadcasted_iota(jnp.int32, (blk, blk), 1)
    visible = (col // CHUNK) <= (row // CHUNK)
    ones = jnp.ones((blk, LANES), BF16)

    def logits(kb, slot, hs=range(hp)):
        start = pl.multiple_of(kb * blk, blk)

        def one(h):
            def run():
                cols = slice(h * MLA_QK_PAD, (h + 1) * MLA_QK_PAD)
                z_ref[slot, h] = _dot_nt(q_ref[0, :, cols], k_ref[0, pl.ds(start, blk), cols])
            return run
        return [one(h) for h in hs]

    def logits2(kb, pair_slot, hs=range(hp)):
        return (logits(jnp.maximum(kb, 0), 2 * pair_slot, hs)
                + logits(jnp.maximum(kb - 1, 0), 2 * pair_slot + 1, hs))

    def v_ones(kb, h):
        start = pl.multiple_of(kb * blk, blk)
        return jnp.concatenate([v_ref[0, pl.ds(start, blk), h * MLA_V:(h + 1) * MLA_V], ones], axis=1)

    def update(h, zs, v1s, first):
        zmax = zs[0] if len(zs) == 1 else jnp.maximum(zs[0], zs[1])
        if first:
            m_new = jnp.broadcast_to(jnp.max(zmax, axis=-1, keepdims=True), (blk, LANES))
        else:
            m = m_ref[h]
            m_new = jnp.maximum(m, jnp.max(zmax, axis=-1, keepdims=True))
        mm = jnp.concatenate([m_new, m_new], axis=1)
        p = jnp.concatenate([jnp.exp2(z - mm).astype(BF16) for z in zs], axis=1)
        pv = _dot(p, v1s[0] if len(zs) == 1 else jnp.concatenate(v1s, axis=0))
        if first:
            acc_ref[h] = pv
        else:
            alpha = jnp.exp2(m - m_new)
            acc_ref[h] = jnp.concatenate([alpha, alpha], axis=1) * acc_ref[h] + pv
        m_ref[h] = m_new

    def double(kb, pair_slot):
        def one(h):
            def run():
                update(h, [z_ref[2 * pair_slot, h], z_ref[2 * pair_slot + 1, h]],
                       [v_ones(kb, h), v_ones(kb - 1, h)], False)
            return run
        return logits2(kb - 2, 1 - pair_slot) + [one(h) for h in range(hp)]

    def single(slot):
        def one(h):
            def run():
                update(h, [z_ref[slot, h]], [v_ones(0, h)], False)
            return run
        return [one(h) for h in range(hp)]

    def diagonal(qi):
        def one(h):
            def run():
                update(h, [jnp.where(visible, z_ref[0, h], -1e30)], [v_ones(qi, h)], True)
            return run
        items = []
        for h in range(hp):
            items += [one(h)] + logits2(qi - 1, 1, [h])
        return items

    return logits, diagonal, double, single


def _self_attn_kernel(sq_ref, sk_ref, sv_ref, mq_ref, mk_ref, mv_ref, so_ref, mo_ref,
                      s_acc, s_run, s_z, m_acc, m_m, m_z, *, blk, hp):
    qi = pl.program_id(2)
    s_logits, s_step = _sb_program(sq_ref, sk_ref, sv_ref, s_acc, s_run, s_z, blk=blk, hp=hp)
    m_logits, m_diag, m_double, m_single = _mla_program(mq_ref, mk_ref, mv_ref, m_acc, m_m, m_z, blk=blk, hp=hp)

    _interleave(s_logits(qi, 0), m_logits(qi, 0))
    _interleave(s_step(qi, 0, True), m_diag(qi))

    def trip(i, carry):
        kb = qi - 1 - 4 * i
        _interleave(s_step(kb, 1, False) + s_step(kb - 1, 0, False), m_double(kb, 1))
        _interleave(s_step(kb - 2, 1, False) + s_step(kb - 3, 0, False), m_double(kb - 2, 0))
        return carry

    trips = lax.shift_right_logical(qi, 2)
    lax.fori_loop(0, trips, trip, 0)
    rem = qi - 4 * trips

    @pl.when(rem >= 2)
    def _():
        _interleave(s_step(rem - 1, 1, False) + s_step(rem - 2, 0, False), m_double(rem - 1, 1))

    @pl.when(rem == 1)
    def _():
        _interleave(s_step(0, 1, False, prefetch=False), m_single(2))

    @pl.when(rem == 3)
    def _():
        _interleave(s_step(0, 1, False, prefetch=False), m_single(0))

    for h in range(hp):
        so_ref[0, :, h * SB_HEAD_DIM:(h + 1) * SB_HEAD_DIM] = s_acc[h].astype(so_ref.dtype)
        mo_ref[0, :, h * MLA_V:(h + 1) * MLA_V] = (
            m_acc[h, :, :MLA_V] / m_acc[h, :, MLA_V:]).astype(mo_ref.dtype)


def _self_attn(qkv, q, k, v, *, heads, blk=256, hp=4):
    B, S, _ = qkv.shape
    groups = heads // hp
    s_wid = hp * SB_HEAD_DIM
    rows = lambda w, off: pl.BlockSpec((1, blk, w), lambda b, g, i: (b, i, off + g))
    full = lambda w, off: pl.BlockSpec((1, S, w), lambda b, g, i: (b, 0, off + g))
    return pl.pallas_call(
        functools.partial(_self_attn_kernel, blk=blk, hp=hp),
        grid=(B, groups, S // blk),
        in_specs=[
            rows(s_wid, 0), full(s_wid, groups), full(s_wid, 2 * groups),
            rows(hp * MLA_QK_PAD, 0), full(hp * MLA_QK_PAD, 0), full(hp * MLA_V, 0),
        ],
        out_specs=[rows(s_wid, 0), rows(hp * MLA_V, 0)],
        out_shape=[jax.ShapeDtypeStruct((B, S, heads * SB_HEAD_DIM), BF16),
                   jax.ShapeDtypeStruct((B, S, heads * MLA_V), BF16)],
        scratch_shapes=[
            pltpu.VMEM((hp, blk, SB_HEAD_DIM), F32), pltpu.VMEM((hp, blk, 1), F32),
            pltpu.VMEM((2, hp, blk, blk), F32),
            pltpu.VMEM((hp, blk, 2 * MLA_V), F32), pltpu.VMEM((hp, blk, LANES), F32),
            pltpu.VMEM((4, hp, blk, blk), F32)],
        compiler_params=pltpu.CompilerParams(
            dimension_semantics=("parallel", "parallel", "arbitrary"), vmem_limit_bytes=VMEM_LIMIT),
        name="self_attn",
    )(qkv, qkv, qkv, q, k, v)


def _mem_block_kernel(x_ref, g_ref, wq_ref, kv_ref, wo_ref, gn_ref, o_ref, hn_ref, *,
                      heads, d_model, q_scale, tn):
    xf = x_ref[0]
    h = (xf * _rms_scale(xf, d_model) * g_ref[...]).astype(BF16)
    hd = d_model // heads
    oms = []
    for i in range(heads):
        cols = slice(i * hd, (i + 1) * hd)
        q = (_dot(h, wq_ref[:, cols]) * q_scale).astype(BF16)
        k = kv_ref[0, :, cols]
        v = kv_ref[0, :, d_model + i * hd:d_model + (i + 1) * hd]
        z = _dot_nt(q, k)
        p = jnp.exp(z - jnp.max(z, axis=-1, keepdims=True))
        l = jnp.sum(p, axis=-1, keepdims=True)
        oms.append((_dot(p.astype(BF16), v) / l).astype(BF16))
    om = jnp.concatenate(oms, axis=1)
    ys = []
    for c in range(0, d_model, tn):
        y = xf[:, c:c + tn] + _dot(om, wo_ref[:, c:c + tn])
        o_ref[0, :, c:c + tn] = y
        ys.append(y)
    y = jnp.concatenate(ys, axis=1)
    hn_ref[0] = (y * _rms_scale(y, d_model) * gn_ref[...]).astype(hn_ref.dtype)


def _mem_block(x, g, wq, kvm, wo, g_next, *, heads, tm=512, tn=512):
    B, S, D = x.shape
    M = kvm.shape[1]
    resident = lambda shape: pl.BlockSpec(shape, lambda b, i: (0, 0), pipeline_mode=pl.Buffered(1))
    vec = pl.BlockSpec((1, D), lambda b, i: (0, 0))
    tile = pl.BlockSpec((1, tm, D), lambda b, i: (b, i, 0))
    return pl.pallas_call(
        functools.partial(_mem_block_kernel, heads=heads, d_model=D,
                          q_scale=(D // heads) ** -0.5, tn=tn),
        grid=(B, S // tm),
        in_specs=[tile, vec, resident((D, D)),
                  pl.BlockSpec((1, M, 2 * D), lambda b, i: (b, 0, 0)), resident((D, D)), vec],
        out_specs=[tile, tile],
        out_shape=[jax.ShapeDtypeStruct((B, S, D), F32), jax.ShapeDtypeStruct((B, S, D), BF16)],
        compiler_params=pltpu.CompilerParams(
            dimension_semantics=("parallel", "parallel"), vmem_limit_bytes=VMEM_LIMIT),
        name="mem_block",
    )(x, g.reshape(1, D), wq, kvm, wo, g_next.reshape(1, D))


def _ffn_kernel(x_ref, h_ref, w1_ref, w2_ref, gf_ref, o_ref, acc_ref, *, d_model):
    f = pl.program_id(1)

    @pl.when(f == 0)
    def _():
        acc_ref[...] = jnp.zeros_like(acc_ref)

    a = jnp.maximum(_dot(h_ref[...], w1_ref[...]), 0.0)
    acc_ref[...] += _dot((a * a).astype(BF16), w2_ref[...])

    @pl.when(f == pl.num_programs(1) - 1)
    def _():
        y = x_ref[...] + acc_ref[...]
        o_ref[...] = y * _rms_scale(y, d_model) * gf_ref[...]


def _ffn(x2d, h2d, w1, w2, gf, *, tm=512, tf=1024):
    T, D = x2d.shape
    F = w1.shape[1]
    return pl.pallas_call(
        functools.partial(_ffn_kernel, d_model=D),
        grid=(T // tm, F // tf),
        in_specs=[
            pl.BlockSpec((tm, D), lambda i, f: (i, 0)),
            pl.BlockSpec((tm, D), lambda i, f: (i, 0)),
            pl.BlockSpec((D, tf), lambda i, f: (0, f)),
            pl.BlockSpec((tf, D), lambda i, f: (f, 0)),
            pl.BlockSpec((1, D), lambda i, f: (0, 0)),
        ],
        out_specs=pl.BlockSpec((tm, D), lambda i, f: (i, 0)),
        out_shape=jax.ShapeDtypeStruct((T, D), F32),
        scratch_shapes=[pltpu.VMEM((tm, D), F32)],
        compiler_params=pltpu.CompilerParams(
            dimension_semantics=("parallel", "arbitrary"), vmem_limit_bytes=VMEM_LIMIT),
        name="ffn",
    )(x2d, h2d, w1, w2, gf.reshape(1, D))


def _pad_cols(w, n):
    return jnp.pad(w, ((0, 0), (0, n - w.shape[1])))


def kernel(x, mem, positions, norm_mix_g, w_in, q_norm_g, w_uq, kv_norm_g, w_ukv, gn_sb_g, gn_mla_g, w_out, norm_mem_g, mem_kv_norm_g, w_mq, w_mkv, w_mo, norm_ffn_g, w_ff1, w_ff2, final_norm_g):
    B, S, D = x.shape
    M = mem.shape[1]
    T = B * S
    depth = w_in.shape[0]
    sb_width = gn_sb_g.shape[1]
    sb_heads = sb_width // SB_HEAD_DIM
    mla_heads = gn_mla_g.shape[1] // MLA_V
    q_lora = q_norm_g.shape[1]
    kv_lora = kv_norm_g.shape[1]

    half = MLA_ROPE // 2
    inv_freq = ROPE_THETA ** (-jnp.arange(half, dtype=F32) / half)
    freq = jnp.concatenate([inv_freq, inv_freq, jnp.zeros((LANES - 2 * half,), F32)]).reshape(1, LANES)
    pos2d = positions.reshape(T, 1)

    xc = x.reshape(T, D)
    for l in range(depth):
        w_in_bf = w_in[l].astype(BF16)
        w_lat = _pad_cols(w_in_bf[:, 3 * sb_width:], q_lora + kv_lora + LANES)
        wq = w_uq[l].reshape(q_lora, mla_heads, MLA_NOPE + MLA_ROPE)
        wq = jnp.pad(wq, ((0, 0), (0, 0), (0, MLA_QK_PAD - MLA_NOPE - MLA_ROPE)))
        wq = wq.reshape(q_lora, mla_heads * MLA_QK_PAD).astype(BF16)
        wkv = w_ukv[l].reshape(kv_lora, mla_heads, MLA_NOPE + MLA_V)
        wuk = wkv[:, :, :MLA_NOPE].reshape(kv_lora, mla_heads * MLA_NOPE).astype(BF16)
        wuv = wkv[:, :, MLA_NOPE:].reshape(kv_lora, mla_heads * MLA_V).astype(BF16)
        sb_scale = jnp.concatenate([
            jnp.full((sb_width,), SB_HEAD_DIM ** -0.5, F32), jnp.ones((2 * sb_width,), F32)])

        qkv, q_mla, k_mla, v_mla = _in_proj(
            xc, norm_mix_g[l], w_in_bf, 3 * sb_width, sb_scale, w_lat, q_norm_g[l], wq, kv_norm_g[l],
            wuk, wuv, pos2d, freq, heads=mla_heads)
        assert sb_heads == mla_heads
        o_sb, o_mla = _self_attn(qkv.reshape(B, S, 3 * sb_width), q_mla.reshape(B, S, -1),
                                 k_mla.reshape(B, S, -1), v_mla.reshape(B, S, -1), heads=sb_heads)
        xc = _norm_matmul(
            [o_sb.reshape(T, -1), o_mla.reshape(T, -1)], [gn_sb_g[l], gn_mla_g[l]],
            w_out[l].astype(BF16), name="out_proj", residual=xc, out_dtype=F32)

        kvm = _norm_matmul([mem.reshape(B * M, D)], [mem_kv_norm_g[l]], w_mkv[l].astype(BF16),
                           name="mem_kv_proj")
        xc, hc = _mem_block(xc.reshape(B, S, D), norm_mem_g[l], w_mq[l].astype(BF16),
                            kvm.reshape(B, M, 2 * D), w_mo[l].astype(BF16), norm_ffn_g[l],
                            heads=MEM_HEADS)

        assert depth == 1
        xc = _ffn(xc.reshape(T, D), hc.reshape(T, D), w_ff1[l].astype(BF16), w_ff2[l].astype(BF16),
                  final_norm_g)
    return xc.reshape(B, S, D)
```

```python
import functools

import jax
import jax.numpy as jnp
from jax import lax
from jax.experimental import pallas as pl
from jax.experimental.pallas import tpu as pltpu

F32 = jnp.float32
BF16 = jnp.bfloat16

EPS = 1e-6
CHUNK = 64
SB_HEAD_DIM = 128
MLA_NOPE = 128
MLA_ROPE = 64
MLA_V = 128
MLA_QK_PAD = 256
ROPE_THETA = 10000.0
LOG2_E = 1.4426950408889634
MEM_HEADS = 4
LANES = 128
VMEM_LIMIT = 56 * 1024 * 1024


def _rms_scale(xf, width):
    return lax.rsqrt(jnp.sum(xf * xf, axis=-1, keepdims=True) * (1.0 / width) + EPS)


def _dot(a, b):
    return jnp.dot(a, b, preferred_element_type=F32)


def _dot_nt(a, b):
    return lax.dot_general(a, b, (((1,), (1,)), ((), ())), preferred_element_type=F32)


def _norm_matmul_kernel(*refs, n_parts, widths, normed, has_scale, has_res, tn):
    pos = 0
    x_refs = refs[pos:pos + n_parts]; pos += n_parts
    g_refs = ()
    if normed:
        g_refs = refs[pos:pos + n_parts]; pos += n_parts
    w_ref = refs[pos]; pos += 1
    scale_ref = None
    if has_scale:
        scale_ref = refs[pos]; pos += 1
    res_ref = None
    if has_res:
        res_ref = refs[pos]; pos += 1
    o_ref = refs[pos]

    hs = []
    for p in range(n_parts):
        xf = x_refs[p][...].astype(F32)
        if normed:
            xf = xf * _rms_scale(xf, widths[p]) * g_refs[p][...]
        hs.append(xf.astype(BF16))
    h = hs[0] if n_parts == 1 else jnp.concatenate(hs, axis=1)
    for c in range(0, o_ref.shape[1], tn):
        acc = _dot(h, w_ref[:, c:c + tn])
        if has_scale:
            acc = acc * scale_ref[:, c:c + tn]
        if has_res:
            acc = acc + res_ref[:, c:c + tn]
        o_ref[:, c:c + tn] = acc.astype(o_ref.dtype)


def _norm_matmul(parts, gains, w, *, name, col_scale=None, residual=None, out_dtype=BF16, tm=512, tn=512):
    T = parts[0].shape[0]
    widths = tuple(p.shape[1] for p in parts)
    K = sum(widths)
    N = w.shape[1]
    tm = min(tm, T)
    tn = min(tn, N)
    assert T % tm == 0 and N % tn == 0 and w.shape[0] == K
    normed = gains is not None
    in_specs = [pl.BlockSpec((tm, k), lambda i: (i, 0)) for k in widths]
    args = list(parts)
    if normed:
        in_specs += [pl.BlockSpec((1, k), lambda i: (0, 0)) for k in widths]
        args += [g.reshape(1, -1).astype(F32) for g in gains]
    in_specs.append(pl.BlockSpec((K, N), lambda i: (0, 0), pipeline_mode=pl.Buffered(1)))
    args.append(w)
    if col_scale is not None:
        in_specs.append(pl.BlockSpec((1, N), lambda i: (0, 0)))
        args.append(col_scale.reshape(1, N).astype(F32))
    if residual is not None:
        in_specs.append(pl.BlockSpec((tm, N), lambda i: (i, 0)))
        args.append(residual)
    kern = functools.partial(
        _norm_matmul_kernel, n_parts=len(parts), widths=widths, normed=normed,
        has_scale=col_scale is not None, has_res=residual is not None, tn=tn)
    return pl.pallas_call(
        kern,
        grid=(T // tm,),
        in_specs=in_specs,
        out_specs=pl.BlockSpec((tm, N), lambda i: (i, 0)),
        out_shape=jax.ShapeDtypeStruct((T, N), out_dtype),
        compiler_params=pltpu.CompilerParams(
            dimension_semantics=("parallel",), vmem_limit_bytes=VMEM_LIMIT),
        name=name,
    )(*args)


def _rope(xp, cos_t, sin_lo, sin_hi):
    half = MLA_ROPE // 2
    return xp * cos_t + pltpu.roll(xp, LANES - half, 1) * sin_lo + pltpu.roll(xp, half, 1) * sin_hi


def _in_proj_kernel(x_ref, g_ref, wsb_ref, sbs_ref, wlat_ref, qg_ref, wuq_ref, kvg_ref, wuk_ref, wuv_ref,
                    pos_ref, freq_ref, qkv_ref, q_ref, k_ref, v_ref, *,
                    d_model, q_lora, kv_lora, heads, q_scale, tn):
    xf = x_ref[...]
    h = (xf * _rms_scale(xf, d_model) * g_ref[...]).astype(BF16)
    lat = _dot_nt(h, wlat_ref[...])
    cq = lat[:, :q_lora]
    ckv = lat[:, q_lora:q_lora + kv_lora]
    kpe = lat[:, q_lora + kv_lora:]

    def sb_chunks(lo, hi):
        for c in range(lo, hi, tn):
            qkv_ref[:, c:c + tn] = (_dot_nt(h, wsb_ref[c:c + tn, :]) * sbs_ref[:, c:c + tn]).astype(qkv_ref.dtype)

    n_sb = qkv_ref.shape[1]
    sb_chunks(0, n_sb // 2)

    half = MLA_ROPE // 2
    ang = pos_ref[...].astype(F32) * freq_ref[...]
    cos_t = jnp.cos(ang)
    sin_t = jnp.sin(ang)
    lane = lax.broadcasted_iota(jnp.int32, ang.shape, 1)
    sin_lo = jnp.where(lane < half, -sin_t, 0.0)
    sin_hi = jnp.where((lane >= half) & (lane < 2 * half), sin_t, 0.0)

    cqn = (cq * _rms_scale(cq, q_lora) * qg_ref[...]).astype(BF16)
    q = _dot(cqn, wuq_ref[...]) * q_scale
    ckvn = (ckv * _rms_scale(ckv, kv_lora) * kvg_ref[...]).astype(BF16)
    kn = _dot(ckvn, wuk_ref[...])
    v_ref[...] = _dot(ckvn, wuv_ref[...]).astype(v_ref.dtype)
    sb_chunks(n_sb // 2, n_sb)
    kpe_r = _rope(kpe, cos_t, sin_lo, sin_hi).astype(k_ref.dtype)
    for hd in range(heads):
        base = hd * MLA_QK_PAD
        q_ref[:, base:base + MLA_NOPE] = q[:, base:base + MLA_NOPE].astype(q_ref.dtype)
        q_ref[:, base + MLA_NOPE:base + MLA_QK_PAD] = _rope(
            q[:, base + MLA_NOPE:base + MLA_QK_PAD], cos_t, sin_lo, sin_hi).astype(q_ref.dtype)
        k_ref[:, base:base + MLA_NOPE] = kn[:, hd * MLA_NOPE:(hd + 1) * MLA_NOPE].astype(k_ref.dtype)
        k_ref[:, base + MLA_NOPE:base + MLA_QK_PAD] = kpe_r


def _in_proj(x2d, g, w_in_bf, n_sb, sb_scale, wlat, qg, wuq, kvg, wuk, wuv, pos2d, freq, *,
             heads, tm=512, tn=512):
    T, D = x2d.shape
    q_lora = qg.shape[0]
    kv_lora = kvg.shape[0]
    nlat = wlat.shape[0]
    vec = lambda n: pl.BlockSpec((1, n), lambda i: (0, 0))
    resident = lambda shape: pl.BlockSpec(shape, lambda i: (0, 0), pipeline_mode=pl.Buffered(1))
    rows = lambda n: pl.BlockSpec((tm, n), lambda i: (i, 0))
    kern = functools.partial(
        _in_proj_kernel, d_model=D, q_lora=q_lora, kv_lora=kv_lora, heads=heads,
        q_scale=LOG2_E * (MLA_NOPE + MLA_ROPE) ** -0.5, tn=tn)
    assert n_sb % (2 * tn) == 0
    return pl.pallas_call(
        kern,
        grid=(T // tm,),
        in_specs=[
            rows(D), vec(D), resident((n_sb, D)), vec(n_sb), resident((nlat, D)),
            vec(q_lora), resident((q_lora, heads * MLA_QK_PAD)),
            vec(kv_lora), resident((kv_lora, heads * MLA_NOPE)), resident((kv_lora, heads * MLA_V)),
            rows(1), vec(LANES),
        ],
        out_specs=[rows(n_sb), rows(heads * MLA_QK_PAD), rows(heads * MLA_QK_PAD), rows(heads * MLA_V)],
        out_shape=[
            jax.ShapeDtypeStruct((T, n_sb), BF16),
            jax.ShapeDtypeStruct((T, heads * MLA_QK_PAD), BF16),
            jax.ShapeDtypeStruct((T, heads * MLA_QK_PAD), BF16),
            jax.ShapeDtypeStruct((T, heads * MLA_V), BF16),
        ],
        compiler_params=pltpu.CompilerParams(
            dimension_semantics=("parallel",), vmem_limit_bytes=VMEM_LIMIT),
        name="in_proj",
    )(x2d, g.reshape(1, D), w_in_bf, sb_scale.reshape(1, n_sb), wlat, qg.reshape(1, -1), wuq,
      kvg.reshape(1, -1), wuk, wuv, pos2d, freq)


def _interleave(a, b):
    merged = sorted([((i + 0.5) / len(a), 0, i) for i in range(len(a))] +
                    [((i + 0.5) / len(b), 1, i) for i in range(len(b))])
    for _, which, i in merged:
        (a, b)[which][i]()


def _sb_program(q_ref, k_ref, v_ref, acc_ref, run_ref, z_ref, *, blk, hp):
    row_chunk = blk
    row = lax.broadcasted_iota(jnp.int32, (blk, blk), 0)
    col = lax.broadcasted_iota(jnp.int32, (blk, blk), 1)
    below = row > col
    suffix = below.astype(BF16)
    heads = [slice(h * SB_HEAD_DIM, (h + 1) * SB_HEAD_DIM) for h in range(hp)]

    def logits(kb, slot):
        start = pl.multiple_of(kb * blk, blk)

        def one(h):
            def run():
                z_ref[slot, h] = _dot_nt(q_ref[0, :, heads[h]], k_ref[0, pl.ds(start, blk), heads[h]])
            return run
        return [one(h) for h in range(hp)]

    def step(kb, slot, masked, prefetch=True):
        start = pl.multiple_of(kb * blk, blk)
        parts = [(h, slice(r, r + row_chunk)) for h in range(hp) for r in range(0, blk, row_chunk)]
        sums = {}

        def suffix_sums(h, rows):
            def run():
                z = z_ref[slot, h, rows, :]
                sp = jnp.maximum(z, 0.0) + jnp.log(1.0 + jnp.exp(-jnp.abs(z)))
                if masked:
                    sp = jnp.where(below[rows], sp, 0.0)
                cs = _dot(sp.astype(BF16), suffix)
                sums[h, rows.start] = (z - sp, cs, cs[:, 0:1] + sp[:, 0:1])
            return run

        def weights_times_v(h, rows):
            def run():
                lb, cs, tot = sums[h, rows.start]
                v = v_ref[0, pl.ds(start, blk), heads[h]]
                if masked:
                    w = jnp.where(below[rows], jnp.exp(lb - cs), 0.0)
                    acc_ref[h, rows, :] = _dot(w.astype(BF16), v)
                    run_ref[h, rows, :] = tot
                else:
                    total = run_ref[h, rows, :]
                    w = jnp.exp(lb - cs - total)
                    acc_ref[h, rows, :] += _dot(w.astype(BF16), v)
                    run_ref[h, rows, :] = total + tot
            return run

        items = logits(jnp.maximum(kb - 1, 0), 1 - slot) if prefetch else []
        return items + [suffix_sums(*p) for p in parts] + [weights_times_v(*p) for p in parts]

    return logits, step


def _mla_program(q_ref, k_ref, v_ref, acc_ref, m_ref, z_ref, *, blk, hp):
    assert MLA_V == LANES
    row = lax.broadcasted_iota(jnp.int32, (blk, blk), 0)
    col = lax.broadcasted_iota(jnp.int32, (blk, blk), 1)
    visible = (col // CHUNK) <= (row // CHUNK)
    ones = jnp.ones((blk, LANES), BF16)

    def logits(kb, slot, hs=range(hp)):
        start = pl.multiple_of(kb * blk, blk)

        def one(h):
            def run():
                cols = slice(h * MLA_QK_PAD, (h + 1) * MLA_QK_PAD)
                z_ref[slot, h] = _dot_nt(q_ref[0, :, cols], k_ref[0, pl.ds(start, blk), cols])
            return run
        return [one(h) for h in hs]

    def logits2(kb, pair_slot, hs=range(hp)):
        return (logits(jnp.maximum(kb, 0), 2 * pair_slot, hs)
                + logits(jnp.maximum(kb - 1, 0), 2 * pair_slot + 1, hs))

    def v_ones(kb, h):
        start = pl.multiple_of(kb * blk, blk)
        return jnp.concatenate([v_ref[0, pl.ds(start, blk), h * MLA_V:(h + 1) * MLA_V], ones], axis=1)

    def update(h, zs, v1s, first):
        zmax = zs[0] if len(zs) == 1 else jnp.maximum(zs[0], zs[1])
        if first:
            m_new = jnp.broadcast_to(jnp.max(zmax, axis=-1, keepdims=True), (blk, LANES))
        else:
            m = m_ref[h]
            m_new = jnp.maximum(m, jnp.max(zmax, axis=-1, keepdims=True))
        mm = jnp.concatenate([m_new, m_new], axis=1)
        p = jnp.concatenate([jnp.exp2(z - mm).astype(BF16) for z in zs], axis=1)
        pv = _dot(p, v1s[0] if len(zs) == 1 else jnp.concatenate(v1s, axis=0))
        if first:
            acc_ref[h] = pv
        else:
            alpha = jnp.exp2(m - m_new)
            acc_ref[h] = jnp.concatenate([alpha, alpha], axis=1) * acc_ref[h] + pv
        m_ref[h] = m_new

    def double(kb, pair_slot):
        def one(h):
            def run():
                update(h, [z_ref[2 * pair_slot, h], z_ref[2 * pair_slot + 1, h]],
                       [v_ones(kb, h), v_ones(kb - 1, h)], False)
            return run
        return logits2(kb - 2, 1 - pair_slot) + [one(h) for h in range(hp)]

    def single(slot):
        def one(h):
            def run():
                update(h, [z_ref[slot, h]], [v_ones(0, h)], False)
            return run
        return [one(h) for h in range(hp)]

    def diagonal(qi):
        def one(h):
            def run():
                update(h, [jnp.where(visible, z_ref[0, h], -1e30)], [v_ones(qi, h)], True)
            return run
        items = []
        for h in range(hp):
            items += [one(h)] + logits2(qi - 1, 1, [h])
        return items

    return logits, diagonal, double, single


def _self_attn_kernel(sq_ref, sk_ref, sv_ref, mq_ref, mk_ref, mv_ref, so_ref, mo_ref,
                      s_acc, s_run, s_z, m_acc, m_m, m_z, *, blk, hp):
    qi = pl.program_id(2)
    s_logits, s_step = _sb_program(sq_ref, sk_ref, sv_ref, s_acc, s_run, s_z, blk=blk, hp=hp)
    m_logits, m_diag, m_double, m_single = _mla_program(mq_ref, mk_ref, mv_ref, m_acc, m_m, m_z, blk=blk, hp=hp)

    _interleave(s_logits(qi, 0), m_logits(qi, 0))
    _interleave(s_step(qi, 0, True), m_diag(qi))

    def trip(i, carry):
        kb = qi - 1 - 4 * i
        _interleave(s_step(kb, 1, False) + s_step(kb - 1, 0, False), m_double(kb, 1))
        _interleave(s_step(kb - 2, 1, False) + s_step(kb - 3, 0, False), m_double(kb - 2, 0))
        return carry

    trips = lax.shift_right_logical(qi, 2)
    lax.fori_loop(0, trips, trip, 0)
    rem = qi - 4 * trips

    @pl.when(rem >= 2)
    def _():
        _interleave(s_step(rem - 1, 1, False) + s_step(rem - 2, 0, False), m_double(rem - 1, 1))

    @pl.when(rem == 1)
    def _():
        _interleave(s_step(0, 1, False, prefetch=False), m_single(2))

    @pl.when(rem == 3)
    def _():
        _interleave(s_step(0, 1, False, prefetch=False), m_single(0))

    for h in range(hp):
        so_ref[0, :, h * SB_HEAD_DIM:(h + 1) * SB_HEAD_DIM] = s_acc[h].astype(so_ref.dtype)
        mo_ref[0, :, h * MLA_V:(h + 1) * MLA_V] = (
            m_acc[h, :, :MLA_V] / m_acc[h, :, MLA_V:]).astype(mo_ref.dtype)


def _self_attn(qkv, q, k, v, *, heads, blk=256, hp=4):
    B, S, _ = qkv.shape
    groups = heads // hp
    s_wid = hp * SB_HEAD_DIM
    rows = lambda w, off: pl.BlockSpec((1, blk, w), lambda b, g, i: (b, i, off + g))
    full = lambda w, off: pl.BlockSpec((1, S, w), lambda b, g, i: (b, 0, off + g))
    return pl.pallas_call(
        functools.partial(_self_attn_kernel, blk=blk, hp=hp),
        grid=(B, groups, S // blk),
        in_specs=[
            rows(s_wid, 0), full(s_wid, groups), full(s_wid, 2 * groups),
            rows(hp * MLA_QK_PAD, 0), full(hp * MLA_QK_PAD, 0), full(hp * MLA_V, 0),
        ],
        out_specs=[rows(s_wid, 0), rows(hp * MLA_V, 0)],
        out_shape=[jax.ShapeDtypeStruct((B, S, heads * SB_HEAD_DIM), BF16),
                   jax.ShapeDtypeStruct((B, S, heads * MLA_V), BF16)],
        scratch_shapes=[
            pltpu.VMEM((hp, blk, SB_HEAD_DIM), F32), pltpu.VMEM((hp, blk, 1), F32),
            pltpu.VMEM((2, hp, blk, blk), F32),
            pltpu.VMEM((hp, blk, 2 * MLA_V), F32), pltpu.VMEM((hp, blk, LANES), F32),
            pltpu.VMEM((4, hp, blk, blk), F32)],
        compiler_params=pltpu.CompilerParams(
            dimension_semantics=("parallel", "parallel", "arbitrary"), vmem_limit_bytes=VMEM_LIMIT),
        name="self_attn",
    )(qkv, qkv, qkv, q, k, v)


def _mem_block_kernel(x_ref, g_ref, wq_ref, kv_ref, wo_ref, gn_ref, o_ref, hn_ref, *,
                      heads, d_model, q_scale, tn):
    xf = x_ref[0]
    h = (xf * _rms_scale(xf, d_model) * g_ref[...]).astype(BF16)
    hd = d_model // heads
    oms = []
    for i in range(heads):
        cols = slice(i * hd, (i + 1) * hd)
        q = (_dot(h, wq_ref[:, cols]) * q_scale).astype(BF16)
        k = kv_ref[0, :, cols]
        v = kv_ref[0, :, d_model + i * hd:d_model + (i + 1) * hd]
        z = _dot_nt(q, k)
        p = jnp.exp(z - jnp.max(z, axis=-1, keepdims=True))
        l = jnp.sum(p, axis=-1, keepdims=True)
        oms.append((_dot(p.astype(BF16), v) / l).astype(BF16))
    om = jnp.concatenate(oms, axis=1)
    ys = []
    for c in range(0, d_model, tn):
        y = xf[:, c:c + tn] + _dot(om, wo_ref[:, c:c + tn])
        o_ref[0, :, c:c + tn] = y
        ys.append(y)
    y = jnp.concatenate(ys, axis=1)
    hn_ref[0] = (y * _rms_scale(y, d_model) * gn_ref[...]).astype(hn_ref.dtype)


def _mem_block(x, g, wq, kvm, wo, g_next, *, heads, tm=512, tn=512):
    B, S, D = x.shape
    M = kvm.shape[1]
    resident = lambda shape: pl.BlockSpec(shape, lambda b, i: (0, 0), pipeline_mode=pl.Buffered(1))
    vec = pl.BlockSpec((1, D), lambda b, i: (0, 0))
    tile = pl.BlockSpec((1, tm, D), lambda b, i: (b, i, 0))
    return pl.pallas_call(
        functools.partial(_mem_block_kernel, heads=heads, d_model=D,
                          q_scale=(D // heads) ** -0.5, tn=tn),
        grid=(B, S // tm),
        in_specs=[tile, vec, resident((D, D)),
                  pl.BlockSpec((1, M, 2 * D), lambda b, i: (b, 0, 0)), resident((D, D)), vec],
        out_specs=[tile, tile],
        out_shape=[jax.ShapeDtypeStruct((B, S, D), F32), jax.ShapeDtypeStruct((B, S, D), BF16)],
        compiler_params=pltpu.CompilerParams(
            dimension_semantics=("parallel", "parallel"), vmem_limit_bytes=VMEM_LIMIT),
        name="mem_block",
    )(x, g.reshape(1, D), wq, kvm, wo, g_next.reshape(1, D))


def _ffn_kernel(x_ref, h_ref, w1_ref, w2_ref, gf_ref, o_ref, acc_ref, *, d_model):
    f = pl.program_id(1)

    @pl.when(f == 0)
    def _():
        acc_ref[...] = jnp.zeros_like(acc_ref)

    a = jnp.maximum(_dot(h_ref[...], w1_ref[...]), 0.0)
    acc_ref[...] += _dot((a * a).astype(BF16), w2_ref[...])

    @pl.when(f == pl.num_programs(1) - 1)
    def _():
        y = x_ref[...] + acc_ref[...]
        o_ref[...] = y * _rms_scale(y, d_model) * gf_ref[...]


def _ffn(x2d, h2d, w1, w2, gf, *, tm=512, tf=1024):
    T, D = x2d.shape
    F = w1.shape[1]
    return pl.pallas_call(
        functools.partial(_ffn_kernel, d_model=D),
        grid=(T // tm, F // tf),
        in_specs=[
            pl.BlockSpec((tm, D), lambda i, f: (i, 0)),
            pl.BlockSpec((tm, D), lambda i, f: (i, 0)),
            pl.BlockSpec((D, tf), lambda i, f: (0, f)),
            pl.BlockSpec((tf, D), lambda i, f: (f, 0)),
            pl.BlockSpec((1, D), lambda i, f: (0, 0)),
        ],
        out_specs=pl.BlockSpec((tm, D), lambda i, f: (i, 0)),
        out_shape=jax.ShapeDtypeStruct((T, D), F32),
        scratch_shapes=[pltpu.VMEM((tm, D), F32)],
        compiler_params=pltpu.CompilerParams(
            dimension_semantics=("parallel", "arbitrary"), vmem_limit_bytes=VMEM_LIMIT),
        name="ffn",
    )(x2d, h2d, w1, w2, gf.reshape(1, D))


def kernel(x, mem, positions, norm_mix_g, w_in, q_norm_g, w_uq, kv_norm_g, w_ukv, gn_sb_g, gn_mla_g, w_out, norm_mem_g, mem_kv_norm_g, w_mq, w_mkv, w_mo, norm_ffn_g, w_ff1, w_ff2, final_norm_g):
    B, S, D = x.shape
    M = mem.shape[1]
    T = B * S
    depth = w_in.shape[0]
    sb_width = gn_sb_g.shape[1]
    sb_heads = sb_width // SB_HEAD_DIM
    mla_heads = gn_mla_g.shape[1] // MLA_V
    q_lora = q_norm_g.shape[1]
    kv_lora = kv_norm_g.shape[1]

    half = MLA_ROPE // 2
    inv_freq = ROPE_THETA ** (-jnp.arange(half, dtype=F32) / half)
    freq = jnp.concatenate([inv_freq, inv_freq, jnp.zeros((LANES - 2 * half,), F32)]).reshape(1, LANES)
    pos2d = positions.reshape(T, 1)

    xc = x.reshape(T, D)
    for l in range(depth):
        w_in_bf = w_in[l].T.astype(BF16)
        w_lat = jnp.pad(w_in_bf[3 * sb_width:], ((0, q_lora + kv_lora + LANES - (w_in_bf.shape[0] - 3 * sb_width)), (0, 0)))
        wq = w_uq[l].reshape(q_lora, mla_heads, MLA_NOPE + MLA_ROPE)
        wq = jnp.pad(wq, ((0, 0), (0, 0), (0, MLA_QK_PAD - MLA_NOPE - MLA_ROPE)))
        wq = wq.reshape(q_lora, mla_heads * MLA_QK_PAD).astype(BF16)
        wkv = w_ukv[l].reshape(kv_lora, mla_heads, MLA_NOPE + MLA_V)
        wuk = wkv[:, :, :MLA_NOPE].reshape(kv_lora, mla_heads * MLA_NOPE).astype(BF16)
        wuv = wkv[:, :, MLA_NOPE:].reshape(kv_lora, mla_heads * MLA_V).astype(BF16)
        sb_scale = jnp.concatenate([
            jnp.full((sb_width,), SB_HEAD_DIM ** -0.5, F32), jnp.ones((2 * sb_width,), F32)])

        qkv, q_mla, k_mla, v_mla = _in_proj(
            xc, norm_mix_g[l], w_in_bf, 3 * sb_width, sb_scale, w_lat, q_norm_g[l], wq, kv_norm_g[l],
            wuk, wuv, pos2d, freq, heads=mla_heads)
        assert sb_heads == mla_heads
        o_sb, o_mla = _self_attn(qkv.reshape(B, S, 3 * sb_width), q_mla.reshape(B, S, -1),
                                 k_mla.reshape(B, S, -1), v_mla.reshape(B, S, -1), heads=sb_heads)
        xc = _norm_matmul(
            [o_sb.reshape(T, -1), o_mla.reshape(T, -1)], [gn_sb_g[l], gn_mla_g[l]],
            w_out[l].astype(BF16), name="out_proj", residual=xc, out_dtype=F32)

        kvm = _norm_matmul([mem.reshape(B * M, D)], [mem_kv_norm_g[l]], w_mkv[l].astype(BF16),
                           name="mem_kv_proj")
        xc, hc = _mem_block(xc.reshape(B, S, D), norm_mem_g[l], w_mq[l].astype(BF16),
                            kvm.reshape(B, M, 2 * D), w_mo[l].astype(BF16), norm_ffn_g[l],
                            heads=MEM_HEADS)

        assert depth == 1
        xc = _ffn(xc.reshape(T, D), hc.reshape(T, D), w_ff1[l].astype(BF16), w_ff2[l].astype(BF16),
                  final_norm_g)
    return xc.reshape(B, S, D)
```

```python
import functools

import jax
import jax.numpy as jnp
from jax import lax
from jax.experimental import pallas as pl
from jax.experimental.pallas import tpu as pltpu

F32 = jnp.float32
BF16 = jnp.bfloat16

EPS = 1e-6
CHUNK = 64
SB_HEAD_DIM = 128
MLA_NOPE = 128
MLA_ROPE = 64
MLA_V = 128
MLA_QK_PAD = 256
ROPE_THETA = 10000.0
LOG2_E = 1.4426950408889634
MEM_HEADS = 4
LANES = 128
VMEM_LIMIT = 56 * 1024 * 1024


def _rms_scale(xf, width):
    return lax.rsqrt(jnp.sum(xf * xf, axis=-1, keepdims=True) * (1.0 / width) + EPS)


def _dot(a, b):
    return jnp.dot(a, b, preferred_element_type=F32)


def _dot_nt(a, b):
    return lax.dot_general(a, b, (((1,), (1,)), ((), ())), preferred_element_type=F32)


def _norm_matmul_kernel(*refs, n_parts, widths, normed, has_scale, has_res, tn):
    pos = 0
    x_refs = refs[pos:pos + n_parts]; pos += n_parts
    g_refs = ()
    if normed:
        g_refs = refs[pos:pos + n_parts]; pos += n_parts
    w_ref = refs[pos]; pos += 1
    scale_ref = None
    if has_scale:
        scale_ref = refs[pos]; pos += 1
    res_ref = None
    if has_res:
        res_ref = refs[pos]; pos += 1
    o_ref = refs[pos]

    hs = []
    for p in range(n_parts):
        xf = x_refs[p][...].astype(F32)
        if normed:
            xf = xf * _rms_scale(xf, widths[p]) * g_refs[p][...]
        hs.append(xf.astype(BF16))
    h = hs[0] if n_parts == 1 else jnp.concatenate(hs, axis=1)
    for c in range(0, o_ref.shape[1], tn):
        acc = _dot(h, w_ref[:, c:c + tn])
        if has_scale:
            acc = acc * scale_ref[:, c:c + tn]
        if has_res:
            acc = acc + res_ref[:, c:c + tn]
        o_ref[:, c:c + tn] = acc.astype(o_ref.dtype)


def _norm_matmul(parts, gains, w, *, name, col_scale=None, residual=None, out_dtype=BF16, tm=512, tn=512):
    T = parts[0].shape[0]
    widths = tuple(p.shape[1] for p in parts)
    K = sum(widths)
    N = w.shape[1]
    tm = min(tm, T)
    tn = min(tn, N)
    assert T % tm == 0 and N % tn == 0 and w.shape[0] == K
    normed = gains is not None
    in_specs = [pl.BlockSpec((tm, k), lambda i: (i, 0)) for k in widths]
    args = list(parts)
    if normed:
        in_specs += [pl.BlockSpec((1, k), lambda i: (0, 0)) for k in widths]
        args += [g.reshape(1, -1).astype(F32) for g in gains]
    in_specs.append(pl.BlockSpec((K, N), lambda i: (0, 0), pipeline_mode=pl.Buffered(1)))
    args.append(w)
    if col_scale is not None:
        in_specs.append(pl.BlockSpec((1, N), lambda i: (0, 0)))
        args.append(col_scale.reshape(1, N).astype(F32))
    if residual is not None:
        in_specs.append(pl.BlockSpec((tm, N), lambda i: (i, 0)))
        args.append(residual)
    kern = functools.partial(
        _norm_matmul_kernel, n_parts=len(parts), widths=widths, normed=normed,
        has_scale=col_scale is not None, has_res=residual is not None, tn=tn)
    return pl.pallas_call(
        kern,
        grid=(T // tm,),
        in_specs=in_specs,
        out_specs=pl.BlockSpec((tm, N), lambda i: (i, 0)),
        out_shape=jax.ShapeDtypeStruct((T, N), out_dtype),
        compiler_params=pltpu.CompilerParams(
            dimension_semantics=("parallel",), vmem_limit_bytes=VMEM_LIMIT),
        name=name,
    )(*args)


def _rope(xp, cos_t, sin_lo, sin_hi):
    half = MLA_ROPE // 2
    return xp * cos_t + pltpu.roll(xp, LANES - half, 1) * sin_lo + pltpu.roll(xp, half, 1) * sin_hi


def _in_proj_kernel(x_ref, g_ref, wsb_ref, sbs_ref, wlat_ref, qg_ref, wuq_ref, kvg_ref, wuk_ref, wuv_ref,
                    pos_ref, freq_ref, qkv_ref, q_ref, k_ref, v_ref, *,
                    d_model, q_lora, kv_lora, heads, q_scale, tn):
    xf = x_ref[...]
    h = (xf * g_ref[...]).astype(BF16)
    row_scale = _rms_scale(xf, d_model)
    lat = _dot_nt(h, wlat_ref[...]) * row_scale
    cq = lat[:, :q_lora]
    ckv = lat[:, q_lora:q_lora + kv_lora]
    kpe = lat[:, q_lora + kv_lora:]

    def sb_chunks(lo, hi):
        for c in range(lo, hi, tn):
            qkv_ref[:, c:c + tn] = (_dot_nt(h, wsb_ref[c:c + tn, :]) * row_scale
                                    * sbs_ref[:, c:c + tn]).astype(qkv_ref.dtype)

    n_sb = qkv_ref.shape[1]
    sb_chunks(0, n_sb // 2)

    half = MLA_ROPE // 2
    ang = pos_ref[...].astype(F32) * freq_ref[...]
    cos_t = jnp.cos(ang)
    sin_t = jnp.sin(ang)
    lane = lax.broadcasted_iota(jnp.int32, ang.shape, 1)
    sin_lo = jnp.where(lane < half, -sin_t, 0.0)
    sin_hi = jnp.where((lane >= half) & (lane < 2 * half), sin_t, 0.0)

    cqn = (cq * _rms_scale(cq, q_lora) * qg_ref[...]).astype(BF16)
    q = _dot(cqn, wuq_ref[...]) * q_scale
    ckvn = (ckv * _rms_scale(ckv, kv_lora) * kvg_ref[...]).astype(BF16)
    kn = _dot(ckvn, wuk_ref[...])
    v_ref[...] = _dot(ckvn, wuv_ref[...]).astype(v_ref.dtype)
    sb_chunks(n_sb // 2, n_sb)
    kpe_r = _rope(kpe, cos_t, sin_lo, sin_hi).astype(k_ref.dtype)
    for hd in range(heads):
        base = hd * MLA_QK_PAD
        q_ref[:, base:base + MLA_NOPE] = q[:, base:base + MLA_NOPE].astype(q_ref.dtype)
        q_ref[:, base + MLA_NOPE:base + MLA_QK_PAD] = _rope(
            q[:, base + MLA_NOPE:base + MLA_QK_PAD], cos_t, sin_lo, sin_hi).astype(q_ref.dtype)
        k_ref[:, base:base + MLA_NOPE] = kn[:, hd * MLA_NOPE:(hd + 1) * MLA_NOPE].astype(k_ref.dtype)
        k_ref[:, base + MLA_NOPE:base + MLA_QK_PAD] = kpe_r


def _in_proj(x2d, g, w_in_bf, n_sb, sb_scale, wlat, qg, wuq, kvg, wuk, wuv, pos2d, freq, *,
             heads, tm=512, tn=512):
    T, D = x2d.shape
    q_lora = qg.shape[0]
    kv_lora = kvg.shape[0]
    nlat = wlat.shape[0]
    vec = lambda n: pl.BlockSpec((1, n), lambda i: (0, 0))
    resident = lambda shape: pl.BlockSpec(shape, lambda i: (0, 0), pipeline_mode=pl.Buffered(1))
    rows = lambda n: pl.BlockSpec((tm, n), lambda i: (i, 0))
    kern = functools.partial(
        _in_proj_kernel, d_model=D, q_lora=q_lora, kv_lora=kv_lora, heads=heads,
        q_scale=LOG2_E * (MLA_NOPE + MLA_ROPE) ** -0.5, tn=tn)
    assert n_sb % (2 * tn) == 0
    return pl.pallas_call(
        kern,
        grid=(T // tm,),
        in_specs=[
            rows(D), vec(D), resident((n_sb, D)), vec(n_sb), resident((nlat, D)),
            vec(q_lora), resident((q_lora, heads * MLA_QK_PAD)),
            vec(kv_lora), resident((kv_lora, heads * MLA_NOPE)), resident((kv_lora, heads * MLA_V)),
            rows(1), vec(LANES),
        ],
        out_specs=[rows(n_sb), rows(heads * MLA_QK_PAD), rows(heads * MLA_QK_PAD), rows(heads * MLA_V)],
        out_shape=[
            jax.ShapeDtypeStruct((T, n_sb), BF16),
            jax.ShapeDtypeStruct((T, heads * MLA_QK_PAD), BF16),
            jax.ShapeDtypeStruct((T, heads * MLA_QK_PAD), BF16),
            jax.ShapeDtypeStruct((T, heads * MLA_V), BF16),
        ],
        compiler_params=pltpu.CompilerParams(
            dimension_semantics=("parallel",), vmem_limit_bytes=VMEM_LIMIT),
        name="in_proj",
    )(x2d, g.reshape(1, D), w_in_bf, sb_scale.reshape(1, n_sb), wlat, qg.reshape(1, -1), wuq,
      kvg.reshape(1, -1), wuk, wuv, pos2d, freq)


def _interleave(a, b):
    merged = sorted([((i + 0.5) / len(a), 0, i) for i in range(len(a))] +
                    [((i + 0.5) / len(b), 1, i) for i in range(len(b))])
    for _, which, i in merged:
        (a, b)[which][i]()


def _sb_program(q_ref, k_ref, v_ref, acc_ref, run_ref, z_ref, *, blk, hp):
    row_chunk = blk
    row = lax.broadcasted_iota(jnp.int32, (blk, blk), 0)
    col = lax.broadcasted_iota(jnp.int32, (blk, blk), 1)
    below = row > col
    suffix = below.astype(BF16)
    heads = [slice(h * SB_HEAD_DIM, (h + 1) * SB_HEAD_DIM) for h in range(hp)]

    def logits(kb, slot):
        start = pl.multiple_of(kb * blk, blk)

        def one(h):
            def run():
                z_ref[slot, h] = _dot_nt(q_ref[0, :, heads[h]], k_ref[0, pl.ds(start, blk), heads[h]])
            return run
        return [one(h) for h in range(hp)]

    def step(kb, slot, masked, prefetch=True):
        start = pl.multiple_of(kb * blk, blk)
        parts = [(h, slice(r, r + row_chunk)) for h in range(hp) for r in range(0, blk, row_chunk)]
        sums = {}

        def suffix_sums(h, rows):
            def run():
                z = z_ref[slot, h, rows, :]
                sp = jnp.maximum(z, 0.0) + jnp.log(1.0 + jnp.exp(-jnp.abs(z)))
                if masked:
                    sp = jnp.where(below[rows], sp, 0.0)
                cs = _dot(sp.astype(BF16), suffix)
                sums[h, rows.start] = (z - sp, cs, cs[:, 0:1] + sp[:, 0:1])
            return run

        def weights_times_v(h, rows):
            def run():
                lb, cs, tot = sums[h, rows.start]
                v = v_ref[0, pl.ds(start, blk), heads[h]]
                if masked:
                    w = jnp.where(below[rows], jnp.exp(lb - cs), 0.0)
                    acc_ref[h, rows, :] = _dot(w.astype(BF16), v)
                    run_ref[h, rows, :] = tot
                else:
                    total = run_ref[h, rows, :]
                    w = jnp.exp(lb - cs - total)
                    acc_ref[h, rows, :] += _dot(w.astype(BF16), v)
                    run_ref[h, rows, :] = total + tot
            return run

        items = logits(jnp.maximum(kb - 1, 0), 1 - slot) if prefetch else []
        return items + [suffix_sums(*p) for p in parts] + [weights_times_v(*p) for p in parts]

    return logits, step


def _mla_program(q_ref, k_ref, v_ref, acc_ref, m_ref, z_ref, *, blk, hp):
    assert MLA_V == LANES
    row = lax.broadcasted_iota(jnp.int32, (blk, blk), 0)
    col = lax.broadcasted_iota(jnp.int32, (blk, blk), 1)
    visible = (col // CHUNK) <= (row // CHUNK)
    ones = jnp.ones((blk, LANES), BF16)

    def logits(kb, slot, hs=range(hp)):
        start = pl.multiple_of(kb * blk, blk)

        def one(h):
            def run():
                cols = slice(h * MLA_QK_PAD, (h + 1) * MLA_QK_PAD)
                z_ref[slot, h] = _dot_nt(q_ref[0, :, cols], k_ref[0, pl.ds(start, blk), cols])
            return run
        return [one(h) for h in hs]

    def logits2(kb, pair_slot, hs=range(hp)):
        return (logits(jnp.maximum(kb, 0), 2 * pair_slot, hs)
                + logits(jnp.maximum(kb - 1, 0), 2 * pair_slot + 1, hs))

    def v_ones(kb, h):
        start = pl.multiple_of(kb * blk, blk)
        return jnp.concatenate([v_ref[0, pl.ds(start, blk), h * MLA_V:(h + 1) * MLA_V], ones], axis=1)

    def update(h, zs, v1s, first):
        zmax = zs[0] if len(zs) == 1 else jnp.maximum(zs[0], zs[1])
        if first:
            m_new = jnp.broadcast_to(jnp.max(zmax, axis=-1, keepdims=True), (blk, LANES))
        else:
            m = m_ref[h]
            m_new = jnp.maximum(m, jnp.max(zmax, axis=-1, keepdims=True))
        mm = jnp.concatenate([m_new, m_new], axis=1)
        p = jnp.concatenate([jnp.exp2(z - mm).astype(BF16) for z in zs], axis=1)
        pv = _dot(p, v1s[0] if len(zs) == 1 else jnp.concatenate(v1s, axis=0))
        if first:
            acc_ref[h] = pv
        else:
            alpha = jnp.exp2(m - m_new)
            acc_ref[h] = jnp.concatenate([alpha, alpha], axis=1) * acc_ref[h] + pv
        m_ref[h] = m_new

    def double(kb, pair_slot):
        def one(h):
            def run():
                update(h, [z_ref[2 * pair_slot, h], z_ref[2 * pair_slot + 1, h]],
                       [v_ones(kb, h), v_ones(kb - 1, h)], False)
            return run
        return logits2(kb - 2, 1 - pair_slot) + [one(h) for h in range(hp)]

    def single(slot):
        def one(h):
            def run():
                update(h, [z_ref[slot, h]], [v_ones(0, h)], False)
            return run
        return [one(h) for h in range(hp)]

    def diagonal(qi):
        def one(h):
            def run():
                update(h, [jnp.where(visible, z_ref[0, h], -1e30)], [v_ones(qi, h)], True)
            return run
        items = []
        for h in range(hp):
            items += [one(h)] + logits2(qi - 1, 1, [h])
        return items

    return logits, diagonal, double, single


def _self_attn_kernel(sq_ref, sk_ref, sv_ref, mq_ref, mk_ref, mv_ref, so_ref, mo_ref,
                      s_acc, s_run, s_z, m_acc, m_m, m_z, *, blk, hp):
    qi = pl.program_id(2)
    s_logits, s_step = _sb_program(sq_ref, sk_ref, sv_ref, s_acc, s_run, s_z, blk=blk, hp=hp)
    m_logits, m_diag, m_double, m_single = _mla_program(mq_ref, mk_ref, mv_ref, m_acc, m_m, m_z, blk=blk, hp=hp)

    _interleave(s_logits(qi, 0), m_logits(qi, 0))
    _interleave(s_step(qi, 0, True), m_diag(qi))

    def trip(i, carry):
        kb = qi - 1 - 4 * i
        _interleave(s_step(kb, 1, False) + s_step(kb - 1, 0, False), m_double(kb, 1))
        _interleave(s_step(kb - 2, 1, False) + s_step(kb - 3, 0, False), m_double(kb - 2, 0))
        return carry

    trips = lax.shift_right_logical(qi, 2)
    lax.fori_loop(0, trips, trip, 0)
    rem = qi - 4 * trips

    @pl.when(rem >= 2)
    def _():
        _interleave(s_step(rem - 1, 1, False) + s_step(rem - 2, 0, False), m_double(rem - 1, 1))

    @pl.when(rem == 1)
    def _():
        _interleave(s_step(0, 1, False, prefetch=False), m_single(2))

    @pl.when(rem == 3)
    def _():
        _interleave(s_step(0, 1, False, prefetch=False), m_single(0))

    for h in range(hp):
        so_ref[0, :, h * SB_HEAD_DIM:(h + 1) * SB_HEAD_DIM] = s_acc[h].astype(so_ref.dtype)
        mo_ref[0, :, h * MLA_V:(h + 1) * MLA_V] = (
            m_acc[h, :, :MLA_V] / m_acc[h, :, MLA_V:]).astype(mo_ref.dtype)


def _self_attn(qkv, q, k, v, *, heads, blk=256, hp=4):
    B, S, _ = qkv.shape
    groups = heads // hp
    s_wid = hp * SB_HEAD_DIM
    rows = lambda w, off: pl.BlockSpec((1, blk, w), lambda b, g, i: (b, i, off + g))
    full = lambda w, off: pl.BlockSpec((1, S, w), lambda b, g, i: (b, 0, off + g))
    return pl.pallas_call(
        functools.partial(_self_attn_kernel, blk=blk, hp=hp),
        grid=(B, groups, S // blk),
        in_specs=[
            rows(s_wid, 0), full(s_wid, groups), full(s_wid, 2 * groups),
            rows(hp * MLA_QK_PAD, 0), full(hp * MLA_QK_PAD, 0), full(hp * MLA_V, 0),
        ],
        out_specs=[rows(s_wid, 0), rows(hp * MLA_V, 0)],
        out_shape=[jax.ShapeDtypeStruct((B, S, heads * SB_HEAD_DIM), BF16),
                   jax.ShapeDtypeStruct((B, S, heads * MLA_V), BF16)],
        scratch_shapes=[
            pltpu.VMEM((hp, blk, SB_HEAD_DIM), F32), pltpu.VMEM((hp, blk, 1), F32),
            pltpu.VMEM((2, hp, blk, blk), F32),
            pltpu.VMEM((hp, blk, 2 * MLA_V), F32), pltpu.VMEM((hp, blk, LANES), F32),
            pltpu.VMEM((4, hp, blk, blk), F32)],
        compiler_params=pltpu.CompilerParams(
            dimension_semantics=("parallel", "parallel", "arbitrary"), vmem_limit_bytes=VMEM_LIMIT),
        name="self_attn",
    )(qkv, qkv, qkv, q, k, v)


def _mem_block_kernel(x_ref, g_ref, wq_ref, kv_ref, wo_ref, gn_ref, o_ref, hn_ref, r2_ref, *,
                      heads, d_model, q_scale, tn):
    xf = x_ref[0]
    h = (xf * g_ref[...]).astype(BF16)
    row_scale = _rms_scale(xf, d_model) * q_scale
    hd = d_model // heads
    oms = []
    for i in range(heads):
        cols = slice(i * hd, (i + 1) * hd)
        q = (_dot(h, wq_ref[:, cols]) * row_scale).astype(BF16)
        k = kv_ref[0, :, cols]
        v = kv_ref[0, :, d_model + i * hd:d_model + (i + 1) * hd]
        z = _dot_nt(q, k)
        p = jnp.exp(z - jnp.max(z, axis=-1, keepdims=True))
        l = jnp.sum(p, axis=-1, keepdims=True)
        oms.append((_dot(p.astype(BF16), v) / l).astype(BF16))
    om = jnp.concatenate(oms, axis=1)
    ssq = jnp.zeros((xf.shape[0], 1), F32)
    for c in range(0, d_model, tn):
        y = xf[:, c:c + tn] + _dot(om, wo_ref[:, c:c + tn])
        o_ref[0, :, c:c + tn] = y
        hn_ref[0, :, c:c + tn] = (y * gn_ref[:, c:c + tn]).astype(hn_ref.dtype)
        ssq = ssq + jnp.sum(y * y, axis=-1, keepdims=True)
    r = lax.rsqrt(ssq * (1.0 / d_model) + EPS)
    r2_ref[0] = r * r


def _mem_block(x, g, wq, kvm, wo, g_next, *, heads, tm=512, tn=512):
    B, S, D = x.shape
    M = kvm.shape[1]
    resident = lambda shape: pl.BlockSpec(shape, lambda b, i: (0, 0), pipeline_mode=pl.Buffered(1))
    vec = pl.BlockSpec((1, D), lambda b, i: (0, 0))
    tile = pl.BlockSpec((1, tm, D), lambda b, i: (b, i, 0))
    return pl.pallas_call(
        functools.partial(_mem_block_kernel, heads=heads, d_model=D,
                          q_scale=(D // heads) ** -0.5, tn=tn),
        grid=(B, S // tm),
        in_specs=[tile, vec, resident((D, D)),
                  pl.BlockSpec((1, M, 2 * D), lambda b, i: (b, 0, 0)), resident((D, D)), vec],
        out_specs=[tile, tile, pl.BlockSpec((1, tm, 1), lambda b, i: (b, i, 0))],
        out_shape=[jax.ShapeDtypeStruct((B, S, D), F32), jax.ShapeDtypeStruct((B, S, D), BF16),
                   jax.ShapeDtypeStruct((B, S, 1), F32)],
        compiler_params=pltpu.CompilerParams(
            dimension_semantics=("parallel", "parallel"), vmem_limit_bytes=VMEM_LIMIT),
        name="mem_block",
    )(x, g.reshape(1, D), wq, kvm, wo, g_next.reshape(1, D))


def _ffn_kernel(x_ref, h_ref, r2_ref, w1_ref, w2_ref, gf_ref, o_ref, acc_ref, *, d_model):
    f = pl.program_id(1)

    @pl.when(f == 0)
    def _():
        acc_ref[...] = jnp.zeros_like(acc_ref)

    a = jnp.maximum(_dot(h_ref[...], w1_ref[...]), 0.0)
    acc_ref[...] += _dot((a * a).astype(BF16), w2_ref[...])

    @pl.when(f == pl.num_programs(1) - 1)
    def _():
        y = x_ref[...] + acc_ref[...] * r2_ref[...]
        o_ref[...] = y * _rms_scale(y, d_model) * gf_ref[...]


def _ffn(x2d, h2d, r2, w1, w2, gf, *, tm=512, tf=1024):
    T, D = x2d.shape
    F = w1.shape[1]
    return pl.pallas_call(
        functools.partial(_ffn_kernel, d_model=D),
        grid=(T // tm, F // tf),
        in_specs=[
            pl.BlockSpec((tm, D), lambda i, f: (i, 0)),
            pl.BlockSpec((tm, D), lambda i, f: (i, 0)),
            pl.BlockSpec((tm, 1), lambda i, f: (i, 0)),
            pl.BlockSpec((D, tf), lambda i, f: (0, f)),
            pl.BlockSpec((tf, D), lambda i, f: (f, 0)),
            pl.BlockSpec((1, D), lambda i, f: (0, 0)),
        ],
        out_specs=pl.BlockSpec((tm, D), lambda i, f: (i, 0)),
        out_shape=jax.ShapeDtypeStruct((T, D), F32),
        scratch_shapes=[pltpu.VMEM((tm, D), F32)],
        compiler_params=pltpu.CompilerParams(
            dimension_semantics=("parallel", "arbitrary"), vmem_limit_bytes=VMEM_LIMIT),
        name="ffn",
    )(x2d, h2d, r2, w1, w2, gf.reshape(1, D))


def kernel(x, mem, positions, norm_mix_g, w_in, q_norm_g, w_uq, kv_norm_g, w_ukv, gn_sb_g, gn_mla_g, w_out, norm_mem_g, mem_kv_norm_g, w_mq, w_mkv, w_mo, norm_ffn_g, w_ff1, w_ff2, final_norm_g):
    B, S, D = x.shape
    M = mem.shape[1]
    T = B * S
    depth = w_in.shape[0]
    sb_width = gn_sb_g.shape[1]
    sb_heads = sb_width // SB_HEAD_DIM
    mla_heads = gn_mla_g.shape[1] // MLA_V
    q_lora = q_norm_g.shape[1]
    kv_lora = kv_norm_g.shape[1]

    half = MLA_ROPE // 2
    inv_freq = ROPE_THETA ** (-jnp.arange(half, dtype=F32) / half)
    freq = jnp.concatenate([inv_freq, inv_freq, jnp.zeros((LANES - 2 * half,), F32)]).reshape(1, LANES)
    pos2d = positions.reshape(T, 1)

    xc = x.reshape(T, D)
    for l in range(depth):
        w_in_bf = w_in[l].T.astype(BF16)
        w_lat = jnp.pad(w_in_bf[3 * sb_width:], ((0, q_lora + kv_lora + LANES - (w_in_bf.shape[0] - 3 * sb_width)), (0, 0)))
        wq = w_uq[l].reshape(q_lora, mla_heads, MLA_NOPE + MLA_ROPE)
        wq = jnp.pad(wq, ((0, 0), (0, 0), (0, MLA_QK_PAD - MLA_NOPE - MLA_ROPE)))
        wq = wq.reshape(q_lora, mla_heads * MLA_QK_PAD).astype(BF16)
        wkv = w_ukv[l].reshape(kv_lora, mla_heads, MLA_NOPE + MLA_V)
        wuk = wkv[:, :, :MLA_NOPE].reshape(kv_lora, mla_heads * MLA_NOPE).astype(BF16)
        wuv = wkv[:, :, MLA_NOPE:].reshape(kv_lora, mla_heads * MLA_V).astype(BF16)
        sb_scale = jnp.concatenate([
            jnp.full((sb_width,), SB_HEAD_DIM ** -0.5, F32), jnp.ones((2 * sb_width,), F32)])

        qkv, q_mla, k_mla, v_mla = _in_proj(
            xc, norm_mix_g[l], w_in_bf, 3 * sb_width, sb_scale, w_lat, q_norm_g[l], wq, kv_norm_g[l],
            wuk, wuv, pos2d, freq, heads=mla_heads)
        assert sb_heads == mla_heads
        o_sb, o_mla = _self_attn(qkv.reshape(B, S, 3 * sb_width), q_mla.reshape(B, S, -1),
                                 k_mla.reshape(B, S, -1), v_mla.reshape(B, S, -1), heads=sb_heads)
        xc = _norm_matmul(
            [o_sb.reshape(T, -1), o_mla.reshape(T, -1)], [gn_sb_g[l], gn_mla_g[l]],
            w_out[l].astype(BF16), name="out_proj", residual=xc, out_dtype=F32)

        kvm = _norm_matmul([mem.reshape(B * M, D)], [mem_kv_norm_g[l]], w_mkv[l].astype(BF16),
                           name="mem_kv_proj")
        xc, hc, r2 = _mem_block(xc.reshape(B, S, D), norm_mem_g[l], w_mq[l].astype(BF16),
                                kvm.reshape(B, M, 2 * D), w_mo[l].astype(BF16), norm_ffn_g[l],
                                heads=MEM_HEADS)

        assert depth == 1
        xc = _ffn(xc.reshape(T, D), hc.reshape(T, D), r2.reshape(T, 1), w_ff1[l].astype(BF16),
                  w_ff2[l].astype(BF16), final_norm_g)
    return xc.reshape(B, S, D)
```
